```python
import math
import jax, jax.numpy as jnp
from jax import lax
import numpy as np

D_MODEL = 1024
BATCH = 8
SEQ = 4096
DEPTH = 1

MIX_WIDTH = D_MODEL
ATT_WIDTH = MIX_WIDTH // 2
POOL_WIDTH = MIX_WIDTH - ATT_WIDTH
HEAD_DIM = 64
N_Q_HEADS = ATT_WIDTH // HEAD_DIM
N_KV_HEADS = 2
Q_PER_KV = N_Q_HEADS // N_KV_HEADS
KV_WIDTH = N_KV_HEADS * HEAD_DIM
Q_BLOCK = 128
GRID_W = 64
AXIS_ROPE_DIM = HEAD_DIM // 2
ROPE_THETA = 10000.0
POOL_WINDOWS = (2, 4, 8, 16)
N_POOL_GROUPS = len(POOL_WINDOWS)
POOL_GROUP_DIM = POOL_WIDTH // N_POOL_GROUPS
IN_PROJ_WIDTH = ATT_WIDTH + 2 * KV_WIDTH + POOL_WIDTH
D_FF = 2816
CONV_W = 3
N_MOD = 6
EPS = 1e-6

kernel_name = "hybrid_gqa_pool_convffn_adaln_block"


def rmsnorm(x, g):
    xf = x.astype(jnp.float32)
    y = xf * lax.rsqrt(jnp.mean(xf * xf, axis=-1, keepdims=True) + EPS)
    return (y * g.astype(jnp.float32)).astype(x.dtype)


def axial_rope_tables(T):
    rows = T // GRID_W
    row = jnp.repeat(jnp.arange(rows), GRID_W).astype(jnp.float32)
    col = jnp.tile(jnp.arange(GRID_W), rows).astype(jnp.float32)
    inv = 1.0 / (ROPE_THETA ** (jnp.arange(0, AXIS_ROPE_DIM, 2, dtype=jnp.float32) / AXIS_ROPE_DIM))
    ang_r = row[:, None] * inv[None, :]
    ang_c = col[:, None] * inv[None, :]
    return jnp.cos(ang_r), jnp.sin(ang_r), jnp.cos(ang_c), jnp.sin(ang_c)


def rotate_chunk(u, cos, sin):
    half = AXIS_ROPE_DIM // 2
    u1, u2 = u[..., :half], u[..., half:]
    cos = cos[None, :, None, :]
    sin = sin[None, :, None, :]
    return jnp.concatenate([u1 * cos - u2 * sin, u2 * cos + u1 * sin], axis=-1)


def apply_axial_rope(u, tables):
    cr, sr, cc, sc = tables
    uf = u.astype(jnp.float32)
    out = jnp.concatenate([rotate_chunk(uf[..., :AXIS_ROPE_DIM], cr, sr),
                           rotate_chunk(uf[..., AXIS_ROPE_DIM:], cc, sc)], axis=-1)
    return out.astype(u.dtype)


def blocked_gqa(q, k, v):
    B, T = q.shape[0], q.shape[1]
    nb = T // Q_BLOCK
    q = q * jnp.asarray(1.0 / math.sqrt(HEAD_DIM), q.dtype)
    qb = q.reshape(B, nb, Q_BLOCK, N_KV_HEADS, Q_PER_KV, HEAD_DIM).transpose(1, 0, 2, 3, 4, 5)

    def one_block(qblk):
        s = jnp.einsum('bqkgd,bskd->bkgqs', qblk, k).astype(jnp.float32)
        p = jax.nn.softmax(s, axis=-1).astype(v.dtype)
        return jnp.einsum('bkgqs,bskd->bqkgd', p, v)

    o = lax.map(one_block, qb)
    return o.transpose(1, 0, 2, 3, 4, 5).reshape(B, T, N_Q_HEADS * HEAD_DIM)


def centred_mean_minus_self(u, w):
    B, T, C = u.shape
    uf = u.astype(jnp.float32)
    S = jnp.concatenate([jnp.zeros((B, 1, C), jnp.float32), jnp.cumsum(uf, axis=1)], axis=1)
    t = jnp.arange(T)
    lo = jnp.maximum(t - w // 2, 0)
    hi = jnp.minimum(t + w // 2 - 1, T - 1)
    win = S[:, hi + 1] - S[:, lo]
    cnt = (hi - lo + 1).astype(jnp.float32)[None, :, None]
    return (win / cnt - uf).astype(u.dtype)


def pool_mixer(u, w_pool, pool_scale):
    B, T, _ = u.shape
    ug = u.reshape(B, T, N_POOL_GROUPS, POOL_GROUP_DIM)
    pooled = jnp.stack([centred_mean_minus_self(ug[:, :, g], POOL_WINDOWS[g])
                        for g in range(N_POOL_GROUPS)], axis=2)
    mixed = jnp.einsum('btgc,gcd->btgd', pooled, w_pool)
    return mixed.reshape(B, T, POOL_WIDTH) * pool_scale


def conv_gated_ffn(h, w_up, conv_w, conv_b, w_down):
    u = h @ w_up
    up = jnp.pad(u, ((0, 0), (1, 1), (0, 0)))
    u = up[:, :-2] * conv_w[0] + up[:, 1:-1] * conv_w[1] + up[:, 2:] * conv_w[2] + conv_b
    gate, val = u[..., :D_FF], u[..., D_FF:]
    return (jax.nn.silu(gate) * val) @ w_down


def setup_inputs(seed: int = 0) -> dict:
    key = jax.random.key(seed)
    ks = jax.random.split(key, 16)
    f32 = jnp.float32
    nrm = lambda k, shape, s: jax.random.normal(k, shape, f32) * s
    return {
        "x": nrm(ks[0], (BATCH, SEQ, D_MODEL), 1.0),
        "c": nrm(ks[1], (BATCH, D_MODEL), 1.0),
        "w_ada": nrm(ks[2], (DEPTH, D_MODEL, N_MOD * D_MODEL), 0.5 * D_MODEL ** -0.5),
        "b_ada": nrm(ks[3], (DEPTH, N_MOD * D_MODEL), 0.01),
        "norm1_g": 1.0 + nrm(ks[4], (DEPTH, D_MODEL), 0.02),
        "w_in": nrm(ks[5], (DEPTH, D_MODEL, IN_PROJ_WIDTH), D_MODEL ** -0.5),
        "q_norm_g": 1.0 + nrm(ks[6], (DEPTH, HEAD_DIM), 0.02),
        "k_norm_g": 1.0 + nrm(ks[7], (DEPTH, HEAD_DIM), 0.02),
        "w_pool": nrm(ks[8], (DEPTH, N_POOL_GROUPS, POOL_GROUP_DIM, POOL_GROUP_DIM), POOL_GROUP_DIM ** -0.5),
        "pool_scale": 1.0 + nrm(ks[9], (DEPTH, POOL_WIDTH), 0.1),
        "w_out": nrm(ks[10], (DEPTH, MIX_WIDTH, D_MODEL), MIX_WIDTH ** -0.5),
        "norm2_g": 1.0 + nrm(ks[11], (DEPTH, D_MODEL), 0.02),
        "w_up": nrm(ks[12], (DEPTH, D_MODEL, 2 * D_FF), D_MODEL ** -0.5),
        "conv_w": nrm(ks[13], (DEPTH, CONV_W, 2 * D_FF), CONV_W ** -0.5),
        "conv_b": nrm(ks[14], (DEPTH, 2 * D_FF), 0.01),
        "w_down": nrm(ks[15], (DEPTH, D_FF, D_MODEL), D_FF ** -0.5),
    }


def reference(x, c, w_ada, b_ada, norm1_g, w_in, q_norm_g, k_norm_g, w_pool, pool_scale,
              w_out, norm2_g, w_up, conv_w, conv_b, w_down):
    B, T, D = x.shape
    tables = axial_rope_tables(T)
    c_act = jax.nn.silu(c)
    for l in range(DEPTH):
        mod = c_act @ w_ada[l] + b_ada[l]
        sh1, sc1, g1, sh2, sc2, g2 = [m[:, None, :] for m in jnp.split(mod, N_MOD, axis=-1)]

        h = rmsnorm(x, norm1_g[l]) * (1.0 + sc1) + sh1
        proj = h @ w_in[l]
        o0 = ATT_WIDTH
        o1 = o0 + KV_WIDTH
        o2 = o1 + KV_WIDTH
        q = proj[..., :o0].reshape(B, T, N_Q_HEADS, HEAD_DIM)
        k = proj[..., o0:o1].reshape(B, T, N_KV_HEADS, HEAD_DIM)
        v = proj[..., o1:o2].reshape(B, T, N_KV_HEADS, HEAD_DIM)
        u_pool = proj[..., o2:]

        q = apply_axial_rope(rmsnorm(q, q_norm_g[l]), tables)
        k = apply_axial_rope(rmsnorm(k, k_norm_g[l]), tables)
        att = blocked_gqa(q, k, v)
        pool = pool_mixer(u_pool, w_pool[l], pool_scale[l])

        mix = jnp.concatenate([att, pool], axis=-1) @ w_out[l]
        x = x + g1 * mix

        h2 = rmsnorm(x, norm2_g[l]) * (1.0 + sc2) + sh2
        x = x + g2 * conv_gated_ffn(h2, w_up[l], conv_w[l], conv_b[l], w_down[l])
    return x
```

```python
import functools
import math

import jax
import jax.numpy as jnp
from jax import lax
from jax.experimental import pallas as pl
from jax.experimental.pallas import tpu as pltpu

HEAD_DIM = 64
N_KV_HEADS = 2
GRID_W = 64
ROPE_THETA = 10000.0
POOL_WINDOWS = (2, 4, 8, 16)
CONV_W = 3
N_MOD = 6
EPS = 1e-6

LANES = 128
HALO = 8
VMEM_LIMIT = 56 * 1024 * 1024

F32 = jnp.float32
BF16 = jnp.bfloat16


def _rms_modulate(xv, g, sc, sh):
    ms = jnp.mean(xv * xv, axis=-1, keepdims=True)
    return (xv * lax.rsqrt(ms + EPS) * g) * (1.0 + sc) + sh


def _halo_specs(tile, n_tiles, width):
    per = tile // HALO
    last = n_tiles * per - 1
    prev = pl.BlockSpec((1, HALO, width), lambda b, i: (b, jnp.maximum(i * per - 1, 0), 0))
    nxt = pl.BlockSpec((1, HALO, width), lambda b, i: (b, jnp.minimum((i + 1) * per, last), 0))
    return prev, nxt


def _mod_kernel(c_ref, w_ref, b_ref, o_ref):
    c = c_ref[...]
    c_act = c * jax.nn.sigmoid(c)
    o_ref[...] = jnp.dot(c_act, w_ref[...], preferred_element_type=F32) + b_ref[...]


def _modulation(c, w_ada, b_ada):
    B, D = c.shape
    n = w_ada.shape[1]
    bn = D
    return pl.pallas_call(
        _mod_kernel,
        grid=(n // bn,),
        in_specs=[pl.BlockSpec((B, D), lambda j: (0, 0)),
                  pl.BlockSpec((D, bn), lambda j: (0, j)),
                  pl.BlockSpec((1, bn), lambda j: (0, j))],
        out_specs=pl.BlockSpec((B, bn), lambda j: (0, j)),
        out_shape=jax.ShapeDtypeStruct((B, n), F32),
        compiler_params=pltpu.CompilerParams(vmem_limit_bytes=VMEM_LIMIT),
        name="adaln_modulation",
    )(c, w_ada, b_ada.reshape(1, n))


def _proj_kernel(x_ref, xp_ref, xn_ref, mod_ref, g1_ref, win_ref, gq_ref, gk_ref, cos_ref, sin_ref,
                 wpool_ref, pscale_ref, qT_ref, k_ref, vT_ref, pool_ref, ubuf, *, seq_len, q_scale):
    i = pl.program_id(1)
    n_i = pl.num_programs(1)
    tt = x_ref.shape[1]
    att_w = qT_ref.shape[1]
    kv_w = N_KV_HEADS * HEAD_DIM
    pool_w = pool_ref.shape[2]
    pool0 = att_w + 2 * kv_w

    sh1 = mod_ref[0, 0:1, :]
    sc1 = mod_ref[0, 1:2, :]
    x_all = jnp.concatenate([xp_ref[0], x_ref[0], xn_ref[0]], axis=0)
    h = _rms_modulate(x_all, g1_ref[...], sc1, sh1).astype(BF16)
    proj = jnp.dot(h, win_ref[...], preferred_element_type=F32)

    lane = lax.broadcasted_iota(jnp.int32, (1, LANES), 1)
    low_head = lane < HEAD_DIM
    first_half = (lane % 32) < 16
    cos = cos_ref[...]
    sin = sin_ref[...]

    def norm_rope(blk, gain):
        sq = blk * blk
        s_lo = jnp.sum(jnp.where(low_head, sq, 0.0), axis=-1, keepdims=True)
        s_hi = jnp.sum(jnp.where(low_head, 0.0, sq), axis=-1, keepdims=True)
        ms = jnp.where(low_head, s_lo, s_hi) * (1.0 / HEAD_DIM)
        y = blk * lax.rsqrt(ms + EPS) * gain
        partner = jnp.where(first_half, pltpu.roll(y, LANES - 16, axis=1), pltpu.roll(y, 16, axis=1))
        return y * cos + partner * sin

    for blk in range(att_w // LANES):
        q = norm_rope(proj[HALO:HALO + tt, blk * LANES:(blk + 1) * LANES], gq_ref[...]) * q_scale
        qT_ref[0, blk * LANES:(blk + 1) * LANES, :] = q.T.astype(BF16)

    kr = norm_rope(proj[HALO:HALO + tt, att_w:att_w + kv_w], gk_ref[...])
    for g in range(N_KV_HEADS):
        k_ref[0, g] = kr[:, g * HEAD_DIM:(g + 1) * HEAD_DIM].astype(BF16)

    vT_ref[0, 0] = proj[HALO:HALO + tt, att_w + kv_w:pool0].T.astype(BF16)

    ubuf[...] = proj[:, pool0:pool0 + pool_w]

    @pl.when(i == 0)
    def _():
        ubuf[0:HALO, :] = jnp.zeros((HALO, pool_w), F32)

    @pl.when(i == n_i - 1)
    def _():
        ubuf[HALO + tt:, :] = jnp.zeros((HALO, pool_w), F32)

    t = i * tt + lax.broadcasted_iota(jnp.int32, (tt, 1), 0)
    gdim = pool_w // len(POOL_WINDOWS)
    for g, w in enumerate(POOL_WINDOWS):
        cs = slice(g * gdim, (g + 1) * gdim)
        win = ubuf[HALO - w // 2:HALO - w // 2 + tt, cs]
        for o in range(-w // 2 + 1, w // 2):
            win = win + ubuf[HALO + o:HALO + o + tt, cs]
        lo = jnp.maximum(t - w // 2, 0)
        hi = jnp.minimum(t + w // 2 - 1, seq_len - 1)
        cnt = (hi - lo + 1).astype(F32)
        pooled = win / cnt - ubuf[HALO:HALO + tt, cs]
        mixed = jnp.dot(pooled.astype(BF16), wpool_ref[g], preferred_element_type=F32)
        pool_ref[0, :, cs] = (mixed * pscale_ref[:, cs]).astype(BF16)


def _projection(x, mod, g1, w_in, gq, gk, cos_t, sin_t, w_pool, pool_scale, *, tt, att_w, q_scale):
    B, T, D = x.shape
    n_t = T // tt
    in_w = w_in.shape[1]
    kv_w = N_KV_HEADS * HEAD_DIM
    pool_w = in_w - att_w - 2 * kv_w
    prev_spec, next_spec = _halo_specs(tt, n_t, D)
    const2 = lambda b, i: (0, 0)
    kern = functools.partial(_proj_kernel, seq_len=T, q_scale=q_scale)
    return pl.pallas_call(
        kern,
        grid=(B, n_t),
        in_specs=[pl.BlockSpec((1, tt, D), lambda b, i: (b, i, 0)),
                  prev_spec, next_spec,
                  pl.BlockSpec((1, N_MOD, D), lambda b, i: (b, 0, 0)),
                  pl.BlockSpec((1, D), const2),
                  pl.BlockSpec((D, in_w), const2),
                  pl.BlockSpec((1, LANES), const2),
                  pl.BlockSpec((1, LANES), const2),
                  pl.BlockSpec((tt, LANES), lambda b, i: (i, 0)),
                  pl.BlockSpec((tt, LANES), lambda b, i: (i, 0)),
                  pl.BlockSpec(w_pool.shape, lambda b, i: (0, 0, 0)),
                  pl.BlockSpec((1, pool_w), const2)],
        out_specs=[pl.BlockSpec((1, att_w, tt), lambda b, i: (b, 0, i)),
                   pl.BlockSpec((1, N_KV_HEADS, tt, HEAD_DIM), lambda b, i: (b, 0, i, 0)),
                   pl.BlockSpec((1, 1, kv_w, tt), lambda b, i: (b, i, 0, 0)),
                   pl.BlockSpec((1, tt, pool_w), lambda b, i: (b, i, 0))],
        out_shape=[jax.ShapeDtypeStruct((B, att_w, T), BF16),
                   jax.ShapeDtypeStruct((B, N_KV_HEADS, T, HEAD_DIM), BF16),
                   jax.ShapeDtypeStruct((B, n_t, kv_w, tt), BF16),
                   jax.ShapeDtypeStruct((B, T, pool_w), BF16)],
        scratch_shapes=[pltpu.VMEM((tt + 2 * HALO, pool_w), F32)],
        compiler_params=pltpu.CompilerParams(vmem_limit_bytes=VMEM_LIMIT),
        name="proj_qkv_pool",
    )(x, x, x, mod, g1, w_in, gq, gk, cos_t, sin_t, w_pool, pool_scale)


def _attn_kernel(qT_ref, k_ref, vT_ref, pool_ref, x_ref, mod_ref, wout_ref, x1_ref, oT_ref):
    n_chunks = vT_ref.shape[1]
    tk = vT_ref.shape[3]
    att_w = qT_ref.shape[1]
    q_per_kv = att_w // HEAD_DIM // N_KV_HEADS

    for g in range(N_KV_HEADS):
        for hh in range(q_per_kv):
            h = g * q_per_kv + hh
            qh = qT_ref[0, h * HEAD_DIM:(h + 1) * HEAD_DIM, :]

            def scores(c):
                ks = k_ref[0, g, pl.ds(pl.multiple_of(c * tk, tk), tk), :]
                return jnp.dot(ks, qh, preferred_element_type=F32)

            def weighted_v(c, p):
                vs = vT_ref[0, c, g * HEAD_DIM:(g + 1) * HEAD_DIM, :]
                return jnp.dot(vs, p.astype(BF16), preferred_element_type=F32)

            s = scores(0)
            m = jnp.max(s, axis=0, keepdims=True)
            p = jnp.exp2(s - m)
            l = jnp.sum(p, axis=0, keepdims=True)
            acc = weighted_v(0, p)

            def body(c, carry):
                m, l, acc = carry
                s = scores(c)
                m_new = jnp.maximum(m, jnp.max(s, axis=0, keepdims=True))
                alpha = jnp.exp2(m - m_new)
                p = jnp.exp2(s - m_new)
                l = alpha * l + jnp.sum(p, axis=0, keepdims=True)
                acc = alpha * acc + weighted_v(c, p)
                return m_new, l, acc

            m, l, acc = lax.fori_loop(1, n_chunks, body, (m, l, acc))
            oT_ref[h * HEAD_DIM:(h + 1) * HEAD_DIM, :] = acc / l

    att = oT_ref[...].T.astype(BF16)
    mix = jnp.dot(att, wout_ref[0:att_w, :], preferred_element_type=F32)
    mix = mix + jnp.dot(pool_ref[0], wout_ref[att_w:, :], preferred_element_type=F32)
    g1 = mod_ref[0, 2:3, :]
    x1_ref[0] = x_ref[0] + g1 * mix


def _attention(qT, k, vT, pool, x, mod, w_out, *, tq):
    B, T, D = x.shape
    att_w = qT.shape[1]
    pool_w = pool.shape[2]
    return pl.pallas_call(
        _attn_kernel,
        grid=(B, T // tq),
        in_specs=[pl.BlockSpec((1, att_w, tq), lambda b, i: (b, 0, i)),
                  pl.BlockSpec((1,) + k.shape[1:], lambda b, i: (b, 0, 0, 0)),
                  pl.BlockSpec((1,) + vT.shape[1:], lambda b, i: (b, 0, 0, 0)),
                  pl.BlockSpec((1, tq, pool_w), lambda b, i: (b, i, 0)),
                  pl.BlockSpec((1, tq, D), lambda b, i: (b, i, 0)),
                  pl.BlockSpec((1, N_MOD, D), lambda b, i: (b, 0, 0)),
                  pl.BlockSpec(w_out.shape, lambda b, i: (0, 0))],
        out_specs=pl.BlockSpec((1, tq, D), lambda b, i: (b, i, 0)),
        out_shape=jax.ShapeDtypeStruct((B, T, D), F32),
        scratch_shapes=[pltpu.VMEM((att_w, tq), F32)],
        compiler_params=pltpu.CompilerParams(vmem_limit_bytes=VMEM_LIMIT),
        name="gqa_outproj",
    )(qT, k, vT, pool, x, mod, w_out)


def _ffn_kernel(x_ref, xp_ref, xn_ref, mod_ref, g2_ref, wup_ref, cw_ref, cb_ref, wdn_ref, o_ref,
                ubuf, act_ref, *, col_chunk):
    i = pl.program_id(1)
    n_i = pl.num_programs(1)
    tf = x_ref.shape[1]
    d_ff = wdn_ref.shape[0]

    sh2 = mod_ref[0, 3:4, :]
    sc2 = mod_ref[0, 4:5, :]
    gate2 = mod_ref[0, 5:6, :]
    x_all = jnp.concatenate([xp_ref[0], x_ref[0], xn_ref[0]], axis=0)
    h = _rms_modulate(x_all, g2_ref[...], sc2, sh2).astype(BF16)

    up_chunk = 2 * col_chunk
    for c0 in range(0, 2 * d_ff, up_chunk):
        ubuf[:, c0:c0 + up_chunk] = jnp.dot(h, wup_ref[:, c0:c0 + up_chunk], preferred_element_type=F32)

    @pl.when(i == 0)
    def _():
        ubuf[0:HALO, :] = jnp.zeros((HALO, 2 * d_ff), F32)

    @pl.when(i == n_i - 1)
    def _():
        ubuf[HALO + tf:, :] = jnp.zeros((HALO, 2 * d_ff), F32)

    def conv(c0):
        cs = slice(c0, c0 + col_chunk)
        y = ubuf[HALO - 1:HALO - 1 + tf, cs] * cw_ref[0:1, cs]
        y = y + ubuf[HALO:HALO + tf, cs] * cw_ref[1:2, cs]
        y = y + ubuf[HALO + 1:HALO + 1 + tf, cs] * cw_ref[2:3, cs]
        return y + cb_ref[:, cs]

    for c0 in range(0, d_ff, col_chunk):
        gate = conv(c0)
        val = conv(d_ff + c0)
        act_ref[:, c0:c0 + col_chunk] = (gate * jax.nn.sigmoid(gate) * val).astype(BF16)

    ffn = jnp.dot(act_ref[...], wdn_ref[...], preferred_element_type=F32)
    o_ref[0] = x_ref[0] + gate2 * ffn


def _ffn(x1, mod, g2, w_up, conv_w, conv_b, w_down, *, tf, col_chunk):
    B, T, D = x1.shape
    n_t = T // tf
    d_ff = w_down.shape[0]
    prev_spec, next_spec = _halo_specs(tf, n_t, D)
    const2 = lambda b, i: (0, 0)
    resident = dict(pipeline_mode=pl.Buffered(1))
    kern = functools.partial(_ffn_kernel, col_chunk=col_chunk)
    return pl.pallas_call(
        kern,
        grid=(B, n_t),
        in_specs=[pl.BlockSpec((1, tf, D), lambda b, i: (b, i, 0)),
                  prev_spec, next_spec,
                  pl.BlockSpec((1, N_MOD, D), lambda b, i: (b, 0, 0)),
                  pl.BlockSpec((1, D), const2),
                  pl.BlockSpec(w_up.shape, const2, **resident),
                  pl.BlockSpec(conv_w.shape, const2),
                  pl.BlockSpec((1, 2 * d_ff), const2),
                  pl.BlockSpec(w_down.shape, const2, **resident)],
        out_specs=pl.BlockSpec((1, tf, D), lambda b, i: (b, i, 0)),
        out_shape=jax.ShapeDtypeStruct((B, T, D), F32),
        scratch_shapes=[pltpu.VMEM((tf + 2 * HALO, 2 * d_ff), F32),
                        pltpu.VMEM((tf, d_ff), BF16)],
        compiler_params=pltpu.CompilerParams(vmem_limit_bytes=VMEM_LIMIT),
        name="conv_gated_ffn",
    )(x1, x1, x1, mod, g2, w_up, conv_w, conv_b, w_down)


def _rope_tables(T):
    axis_dim = HEAD_DIM // 2
    pos = jnp.arange(T)
    row = (pos // GRID_W).astype(F32)
    col = (pos % GRID_W).astype(F32)
    d = jnp.arange(LANES) % HEAD_DIM
    inv = 1.0 / (ROPE_THETA ** (jnp.arange(0, axis_dim, 2, dtype=F32) / axis_dim))
    freq = inv[d % (axis_dim // 2)]
    p = jnp.where((d // axis_dim == 0)[None, :], row[:, None], col[:, None])
    ang = p * freq[None, :]
    sign = jnp.where((d % axis_dim) < axis_dim // 2, -1.0, 1.0).astype(F32)
    return jnp.cos(ang), jnp.sin(ang) * sign[None, :]


def kernel(x, c, w_ada, b_ada, norm1_g, w_in, q_norm_g, k_norm_g, w_pool, pool_scale, w_out, norm2_g,
           w_up, conv_w, conv_b, w_down):
    B, T, D = x.shape
    depth = w_ada.shape[0]
    att_w = D // 2
    tile = 512
    assert T % tile == 0 and T % GRID_W == 0 and D % LANES == 0
    d_ff = w_down.shape[1]
    q_scale = math.log2(math.e) / math.sqrt(HEAD_DIM)
    cos_t, sin_t = _rope_tables(T)
    heads_per_block = LANES // HEAD_DIM

    for l in range(depth):
        mod = _modulation(c, w_ada[l], b_ada[l]).reshape(B, N_MOD, D)
        qT, k, vT, pool = _projection(
            x, mod, norm1_g[l].reshape(1, D), w_in[l].astype(BF16),
            jnp.tile(q_norm_g[l], heads_per_block).reshape(1, LANES),
            jnp.tile(k_norm_g[l], heads_per_block).reshape(1, LANES),
            cos_t, sin_t, w_pool[l].astype(BF16), pool_scale[l].reshape(1, -1),
            tt=tile, att_w=att_w, q_scale=q_scale)
        x1 = _attention(qT, k, vT, pool, x, mod, w_out[l].astype(BF16), tq=tile)
        x = _ffn(x1, mod, norm2_g[l].reshape(1, D), w_up[l].astype(BF16), conv_w[l],
                 conv_b[l].reshape(1, -1), w_down[l].astype(BF16), tf=tile, col_chunk=256)
    return x
```

```python
import functools
import math

import jax
import jax.numpy as jnp
from jax import lax
from jax.experimental import pallas as pl
from jax.experimental.pallas import tpu as pltpu

HEAD_DIM = 64
N_KV_HEADS = 2
GRID_W = 64
ROPE_THETA = 10000.0
POOL_WINDOWS = (2, 4, 8, 16)
CONV_W = 3
N_MOD = 6
EPS = 1e-6

LANES = 128
HALO = 8
VMEM_LIMIT = 56 * 1024 * 1024
SCORE_LOOKAHEAD = 2

F32 = jnp.float32
BF16 = jnp.bfloat16


def _rms_modulate(xv, g, sc, sh):
    ms = jnp.mean(xv * xv, axis=-1, keepdims=True)
    return (xv * lax.rsqrt(ms + EPS) * g) * (1.0 + sc) + sh


def _halo_specs(tile, n_tiles, width):
    per = tile // HALO
    last = n_tiles * per - 1
    prev = pl.BlockSpec((1, HALO, width), lambda b, i: (b, jnp.maximum(i * per - 1, 0), 0))
    nxt = pl.BlockSpec((1, HALO, width), lambda b, i: (b, jnp.minimum((i + 1) * per, last), 0))
    return prev, nxt


def _mod_kernel(c_ref, w_ref, b_ref, o_ref):
    c = c_ref[...]
    c_act = c * jax.nn.sigmoid(c)
    o_ref[...] = jnp.dot(c_act, w_ref[...], preferred_element_type=F32) + b_ref[...]


def _modulation(c, w_ada, b_ada):
    B, D = c.shape
    n = w_ada.shape[1]
    bn = D
    return pl.pallas_call(
        _mod_kernel,
        grid=(n // bn,),
        in_specs=[pl.BlockSpec((B, D), lambda j: (0, 0)),
                  pl.BlockSpec((D, bn), lambda j: (0, j)),
                  pl.BlockSpec((1, bn), lambda j: (0, j))],
        out_specs=pl.BlockSpec((B, bn), lambda j: (0, j)),
        out_shape=jax.ShapeDtypeStruct((B, n), F32),
        compiler_params=pltpu.CompilerParams(vmem_limit_bytes=VMEM_LIMIT),
        name="adaln_modulation",
    )(c, w_ada, b_ada.reshape(1, n))


def _proj_kernel(x_ref, xp_ref, xn_ref, mod_ref, g1_ref, win_ref, gq_ref, gk_ref, cos_ref, sin_ref,
                 wpool_ref, pscale_ref, qT_ref, k_ref, vT_ref, pool_ref, ubuf, *, seq_len, q_scale):
    i = pl.program_id(1)
    n_i = pl.num_programs(1)
    tt = x_ref.shape[1]
    att_w = qT_ref.shape[1]
    kv_w = N_KV_HEADS * HEAD_DIM
    pool_w = pool_ref.shape[2]
    pool0 = att_w + 2 * kv_w

    sh1 = mod_ref[0, 0:1, :]
    sc1 = mod_ref[0, 1:2, :]
    x_all = jnp.concatenate([xp_ref[0], x_ref[0], xn_ref[0]], axis=0)
    h = _rms_modulate(x_all, g1_ref[...], sc1, sh1).astype(BF16)
    proj = jnp.dot(h, win_ref[...], preferred_element_type=F32)

    lane = lax.broadcasted_iota(jnp.int32, (1, LANES), 1)
    low_head = lane < HEAD_DIM
    first_half = (lane % 32) < 16
    cos = cos_ref[...]
    sin = sin_ref[...]

    def norm_rope(blk, gain):
        sq = blk * blk
        s_lo = jnp.sum(jnp.where(low_head, sq, 0.0), axis=-1, keepdims=True)
        s_hi = jnp.sum(jnp.where(low_head, 0.0, sq), axis=-1, keepdims=True)
        ms = jnp.where(low_head, s_lo, s_hi) * (1.0 / HEAD_DIM)
        y = blk * lax.rsqrt(ms + EPS) * gain
        partner = jnp.where(first_half, pltpu.roll(y, LANES - 16, axis=1), pltpu.roll(y, 16, axis=1))
        return y * cos + partner * sin

    for blk in range(att_w // LANES):
        q = norm_rope(proj[HALO:HALO + tt, blk * LANES:(blk + 1) * LANES], gq_ref[...]) * q_scale
        qT_ref[0, blk * LANES:(blk + 1) * LANES, :] = q.T.astype(BF16)

    kr = norm_rope(proj[HALO:HALO + tt, att_w:att_w + kv_w], gk_ref[...])
    for g in range(N_KV_HEADS):
        k_ref[0, g] = kr[:, g * HEAD_DIM:(g + 1) * HEAD_DIM].astype(BF16)

    vT_ref[0, 0] = proj[HALO:HALO + tt, att_w + kv_w:pool0].T.astype(BF16)

    ubuf[...] = proj[:, pool0:pool0 + pool_w]

    @pl.when(i == 0)
    def _():
        ubuf[0:HALO, :] = jnp.zeros((HALO, pool_w), F32)

    @pl.when(i == n_i - 1)
    def _():
        ubuf[HALO + tt:, :] = jnp.zeros((HALO, pool_w), F32)

    t = i * tt + lax.broadcasted_iota(jnp.int32, (tt, 1), 0)
    gdim = pool_w // len(POOL_WINDOWS)
    for g, w in enumerate(POOL_WINDOWS):
        cs = slice(g * gdim, (g + 1) * gdim)
        win = ubuf[HALO - w // 2:HALO - w // 2 + tt, cs]
        for o in range(-w // 2 + 1, w // 2):
            win = win + ubuf[HALO + o:HALO + o + tt, cs]
        lo = jnp.maximum(t - w // 2, 0)
        hi = jnp.minimum(t + w // 2 - 1, seq_len - 1)
        cnt = (hi - lo + 1).astype(F32)
        pooled = win / cnt - ubuf[HALO:HALO + tt, cs]
        mixed = jnp.dot(pooled.astype(BF16), wpool_ref[g], preferred_element_type=F32)
        pool_ref[0, :, cs] = (mixed * pscale_ref[:, cs]).astype(BF16)


def _projection(x, mod, g1, w_in, gq, gk, cos_t, sin_t, w_pool, pool_scale, *, tt, att_w, q_scale):
    B, T, D = x.shape
    n_t = T // tt
    in_w = w_in.shape[1]
    kv_w = N_KV_HEADS * HEAD_DIM
    pool_w = in_w - att_w - 2 * kv_w
    prev_spec, next_spec = _halo_specs(tt, n_t, D)
    const2 = lambda b, i: (0, 0)
    kern = functools.partial(_proj_kernel, seq_len=T, q_scale=q_scale)
    return pl.pallas_call(
        kern,
        grid=(B, n_t),
        in_specs=[pl.BlockSpec((1, tt, D), lambda b, i: (b, i, 0)),
                  prev_spec, next_spec,
                  pl.BlockSpec((1, N_MOD, D), lambda b, i: (b, 0, 0)),
                  pl.BlockSpec((1, D), const2),
                  pl.BlockSpec((D, in_w), const2),
                  pl.BlockSpec((1, LANES), const2),
                  pl.BlockSpec((1, LANES), const2),
                  pl.BlockSpec((tt, LANES), lambda b, i: (i, 0)),
                  pl.BlockSpec((tt, LANES), lambda b, i: (i, 0)),
                  pl.BlockSpec(w_pool.shape, lambda b, i: (0, 0, 0)),
                  pl.BlockSpec((1, pool_w), const2)],
        out_specs=[pl.BlockSpec((1, att_w, tt), lambda b, i: (b, 0, i)),
                   pl.BlockSpec((1, N_KV_HEADS, tt, HEAD_DIM), lambda b, i: (b, 0, i, 0)),
                   pl.BlockSpec((1, 1, kv_w, tt), lambda b, i: (b, i, 0, 0)),
                   pl.BlockSpec((1, tt, pool_w), lambda b, i: (b, i, 0))],
        out_shape=[jax.ShapeDtypeStruct((B, att_w, T), BF16),
                   jax.ShapeDtypeStruct((B, N_KV_HEADS, T, HEAD_DIM), BF16),
                   jax.ShapeDtypeStruct((B, n_t, kv_w, tt), BF16),
                   jax.ShapeDtypeStruct((B, T, pool_w), BF16)],
        scratch_shapes=[pltpu.VMEM((tt + 2 * HALO, pool_w), F32)],
        compiler_params=pltpu.CompilerParams(vmem_limit_bytes=VMEM_LIMIT),
        name="proj_qkv_pool",
    )(x, x, x, mod, g1, w_in, gq, gk, cos_t, sin_t, w_pool, pool_scale)


def _attn_kernel(qT_ref, k_ref, vT_ref, pool_ref, x_ref, mod_ref, wout_ref, x1_ref, oT_ref):
    n_chunks = vT_ref.shape[1]
    tk = vT_ref.shape[3]
    att_w = qT_ref.shape[1]
    q_per_kv = att_w // HEAD_DIM // N_KV_HEADS

    n_heads = att_w // HEAD_DIM
    m_run = [None] * n_heads
    l_run = [None] * n_heads

    def scores(c, h):
        g = h // q_per_kv
        ks = k_ref[0, g, c * tk:(c + 1) * tk, :]
        rows = slice(h * HEAD_DIM, (h + 1) * HEAD_DIM)
        return jnp.dot(ks, qT_ref[0, rows, :], preferred_element_type=F32)

    def accumulate(c, h, s):
        g = h // q_per_kv
        rows = slice(h * HEAD_DIM, (h + 1) * HEAD_DIM)
        vs = vT_ref[0, c, g * HEAD_DIM:(g + 1) * HEAD_DIM, :]
        m_c = jnp.max(s, axis=0, keepdims=True)
        if m_run[h] is None:
            m_new = m_c
            p = jnp.exp2(s - m_new)
            l_run[h] = jnp.sum(p, axis=0, keepdims=True)
            oT_ref[rows, :] = jnp.dot(vs, p.astype(BF16), preferred_element_type=F32)
        else:
            m_new = jnp.maximum(m_run[h], m_c)
            alpha = jnp.exp2(m_run[h] - m_new)
            p = jnp.exp2(s - m_new)
            l_run[h] = alpha * l_run[h] + jnp.sum(p, axis=0, keepdims=True)
            pv = jnp.dot(vs, p.astype(BF16), preferred_element_type=F32)
            oT_ref[rows, :] = alpha * oT_ref[rows, :] + pv
        m_run[h] = m_new

    stages = [(c, g * q_per_kv + hh) for g in range(N_KV_HEADS) for c in range(n_chunks)
              for hh in range(q_per_kv)]
    in_flight = []
    for c, h in stages:
        in_flight.append((c, h, scores(c, h)))
        if len(in_flight) > SCORE_LOOKAHEAD:
            accumulate(*in_flight.pop(0))
    for item in in_flight:
        accumulate(*item)

    for h in range(n_heads):
        rows = slice(h * HEAD_DIM, (h + 1) * HEAD_DIM)
        oT_ref[rows, :] = oT_ref[rows, :] / l_run[h]

    att = oT_ref[...].T.astype(BF16)
    mix = jnp.dot(att, wout_ref[0:att_w, :], preferred_element_type=F32)
    mix = mix + jnp.dot(pool_ref[0], wout_ref[att_w:, :], preferred_element_type=F32)
    g1 = mod_ref[0, 2:3, :]
    x1_ref[0] = x_ref[0] + g1 * mix


def _attention(qT, k, vT, pool, x, mod, w_out, *, tq):
    B, T, D = x.shape
    att_w = qT.shape[1]
    pool_w = pool.shape[2]
    return pl.pallas_call(
        _attn_kernel,
        grid=(B, T // tq),
        in_specs=[pl.BlockSpec((1, att_w, tq), lambda b, i: (b, 0, i)),
                  pl.BlockSpec((1,) + k.shape[1:], lambda b, i: (b, 0, 0, 0)),
                  pl.BlockSpec((1,) + vT.shape[1:], lambda b, i: (b, 0, 0, 0)),
                  pl.BlockSpec((1, tq, pool_w), lambda b, i: (b, i, 0)),
                  pl.BlockSpec((1, tq, D), lambda b, i: (b, i, 0)),
                  pl.BlockSpec((1, N_MOD, D), lambda b, i: (b, 0, 0)),
                  pl.BlockSpec(w_out.shape, lambda b, i: (0, 0))],
        out_specs=pl.BlockSpec((1, tq, D), lambda b, i: (b, i, 0)),
        out_shape=jax.ShapeDtypeStruct((B, T, D), F32),
        scratch_shapes=[pltpu.VMEM((att_w, tq), F32)],
        compiler_params=pltpu.CompilerParams(vmem_limit_bytes=VMEM_LIMIT),
        name="gqa_outproj",
    )(qT, k, vT, pool, x, mod, w_out)


def _ffn_kernel(x_ref, xp_ref, xn_ref, mod_ref, g2_ref, wup_ref, cw_ref, cb_ref, wdn_ref, o_ref,
                ubuf, act_ref, *, col_chunk):
    i = pl.program_id(1)
    n_i = pl.num_programs(1)
    tf = x_ref.shape[1]
    d_ff = wdn_ref.shape[0]

    sh2 = mod_ref[0, 3:4, :]
    sc2 = mod_ref[0, 4:5, :]
    gate2 = mod_ref[0, 5:6, :]
    x_all = jnp.concatenate([xp_ref[0], x_ref[0], xn_ref[0]], axis=0)
    h = _rms_modulate(x_all, g2_ref[...], sc2, sh2).astype(BF16)

    up_chunk = 2 * col_chunk
    for c0 in range(0, 2 * d_ff, up_chunk):
        ubuf[:, c0:c0 + up_chunk] = jnp.dot(h, wup_ref[:, c0:c0 + up_chunk], preferred_element_type=F32)

    @pl.when(i == 0)
    def _():
        ubuf[0:HALO, :] = jnp.zeros((HALO, 2 * d_ff), F32)

    @pl.when(i == n_i - 1)
    def _():
        ubuf[HALO + tf:, :] = jnp.zeros((HALO, 2 * d_ff), F32)

    def conv(c0):
        cs = slice(c0, c0 + col_chunk)
        y = ubuf[HALO - 1:HALO - 1 + tf, cs] * cw_ref[0:1, cs]
        y = y + ubuf[HALO:HALO + tf, cs] * cw_ref[1:2, cs]
        y = y + ubuf[HALO + 1:HALO + 1 + tf, cs] * cw_ref[2:3, cs]
        return y + cb_ref[:, cs]

    for c0 in range(0, d_ff, col_chunk):
        gate = conv(c0)
        val = conv(d_ff + c0)
        act_ref[:, c0:c0 + col_chunk] = (gate * jax.nn.sigmoid(gate) * val).astype(BF16)

    ffn = jnp.dot(act_ref[...], wdn_ref[...], preferred_element_type=F32)
    o_ref[0] = x_ref[0] + gate2 * ffn


def _ffn(x1, mod, g2, w_up, conv_w, conv_b, w_down, *, tf, col_chunk):
    B, T, D = x1.shape
    n_t = T // tf
    d_ff = w_down.shape[0]
    prev_spec, next_spec = _halo_specs(tf, n_t, D)
    const2 = lambda b, i: (0, 0)
    resident = dict(pipeline_mode=pl.Buffered(1))
    kern = functools.partial(_ffn_kernel, col_chunk=col_chunk)
    return pl.pallas_call(
        kern,
        grid=(B, n_t),
        in_specs=[pl.BlockSpec((1, tf, D), lambda b, i: (b, i, 0)),
                  prev_spec, next_spec,
                  pl.BlockSpec((1, N_MOD, D), lambda b, i: (b, 0, 0)),
                  pl.BlockSpec((1, D), const2),
                  pl.BlockSpec(w_up.shape, const2, **resident),
                  pl.BlockSpec(conv_w.shape, const2),
                  pl.BlockSpec((1, 2 * d_ff), const2),
                  pl.BlockSpec(w_down.shape, const2, **resident)],
        out_specs=pl.BlockSpec((1, tf, D), lambda b, i: (b, i, 0)),
        out_shape=jax.ShapeDtypeStruct((B, T, D), F32),
        scratch_shapes=[pltpu.VMEM((tf + 2 * HALO, 2 * d_ff), F32),
                        pltpu.VMEM((tf, d_ff), BF16)],
        compiler_params=pltpu.CompilerParams(vmem_limit_bytes=VMEM_LIMIT),
        name="conv_gated_ffn",
    )(x1, x1, x1, mod, g2, w_up, conv_w, conv_b, w_down)


def _rope_tables(T):
    axis_dim = HEAD_DIM // 2
    pos = jnp.arange(T)
    row = (pos // GRID_W).astype(F32)
    col = (pos % GRID_W).astype(F32)
    d = jnp.arange(LANES) % HEAD_DIM
    inv = 1.0 / (ROPE_THETA ** (jnp.arange(0, axis_dim, 2, dtype=F32) / axis_dim))
    freq = inv[d % (axis_dim // 2)]
    p = jnp.where((d // axis_dim == 0)[None, :], row[:, None], col[:, None])
    ang = p * freq[None, :]
    sign = jnp.where((d % axis_dim) < axis_dim // 2, -1.0, 1.0).astype(F32)
    return jnp.cos(ang), jnp.sin(ang) * sign[None, :]


def kernel(x, c, w_ada, b_ada, norm1_g, w_in, q_norm_g, k_norm_g, w_pool, pool_scale, w_out, norm2_g,
           w_up, conv_w, conv_b, w_down):
    B, T, D = x.shape
    depth = w_ada.shape[0]
    att_w = D // 2
    tile = 512
    assert T % tile == 0 and T % GRID_W == 0 and D % LANES == 0
    d_ff = w_down.shape[1]
    q_scale = math.log2(math.e) / math.sqrt(HEAD_DIM)
    cos_t, sin_t = _rope_tables(T)
    heads_per_block = LANES // HEAD_DIM

    for l in range(depth):
        mod = _modulation(c, w_ada[l], b_ada[l]).reshape(B, N_MOD, D)
        qT, k, vT, pool = _projection(
            x, mod, norm1_g[l].reshape(1, D), w_in[l].astype(BF16),
            jnp.tile(q_norm_g[l], heads_per_block).reshape(1, LANES),
            jnp.tile(k_norm_g[l], heads_per_block).reshape(1, LANES),
            cos_t, sin_t, w_pool[l].astype(BF16), pool_scale[l].reshape(1, -1),
            tt=tile, att_w=att_w, q_scale=q_scale)
        x1 = _attention(qT, k, vT, pool, x, mod, w_out[l].astype(BF16), tq=tile)
        x = _ffn(x1, mod, norm2_g[l].reshape(1, D), w_up[l].astype(BF16), conv_w[l],
                 conv_b[l].reshape(1, -1), w_down[l].astype(BF16), tf=tile, col_chunk=256)
    return x
```

```python
import functools
import math

import jax
import jax.numpy as jnp
from jax import lax
from jax.experimental import pallas as pl
from jax.experimental.pallas import tpu as pltpu

HEAD_DIM = 64
N_KV_HEADS = 2
GRID_W = 64
ROPE_THETA = 10000.0
POOL_WINDOWS = (2, 4, 8, 16)
CONV_W = 3
N_MOD = 6
EPS = 1e-6

LANES = 128
HALO = 8
VMEM_LIMIT = 56 * 1024 * 1024
SCORE_LOOKAHEAD = 2

F32 = jnp.float32
BF16 = jnp.bfloat16


def _rms_modulate(xv, g, sc, sh):
    ms = jnp.mean(xv * xv, axis=-1, keepdims=True)
    return (xv * lax.rsqrt(ms + EPS) * g) * (1.0 + sc) + sh


def _halo_specs(tile, n_tiles, width):
    per = tile // HALO
    last = n_tiles * per - 1
    prev = pl.BlockSpec((1, HALO, width), lambda b, i: (b, jnp.maximum(i * per - 1, 0), 0))
    nxt = pl.BlockSpec((1, HALO, width), lambda b, i: (b, jnp.minimum((i + 1) * per, last), 0))
    return prev, nxt


def _mod_kernel(c_ref, w_ref, b_ref, o_ref):
    c = c_ref[...]
    c_act = c * jax.nn.sigmoid(c)
    o_ref[...] = jnp.dot(c_act, w_ref[...], preferred_element_type=F32) + b_ref[...]


def _modulation(c, w_ada, b_ada):
    B, D = c.shape
    n = w_ada.shape[1]
    bn = D
    return pl.pallas_call(
        _mod_kernel,
        grid=(n // bn,),
        in_specs=[pl.BlockSpec((B, D), lambda j: (0, 0)),
                  pl.BlockSpec((D, bn), lambda j: (0, j)),
                  pl.BlockSpec((1, bn), lambda j: (0, j))],
        out_specs=pl.BlockSpec((B, bn), lambda j: (0, j)),
        out_shape=jax.ShapeDtypeStruct((B, n), F32),
        compiler_params=pltpu.CompilerParams(vmem_limit_bytes=VMEM_LIMIT),
        name="adaln_modulation",
    )(c, w_ada, b_ada.reshape(1, n))


def _proj_kernel(x_ref, xp_ref, xn_ref, mod_ref, g1_ref, win_ref, gq_ref, gk_ref, cos_ref, sin_ref,
                 wpool_ref, pscale_ref, qT_ref, k_ref, vT_ref, pool_ref, u_ref, y_ref, *, seq_len):
    i = pl.program_id(1)
    n_i = pl.num_programs(1)
    tt = x_ref.shape[1]
    half = tt // 2
    att_w = qT_ref.shape[1]
    kv_w = N_KV_HEADS * HEAD_DIM
    pool_w = pool_ref.shape[2]
    pool0 = att_w + 2 * kv_w

    sh1 = mod_ref[0, 0:1, :]
    sc1 = mod_ref[0, 1:2, :]
    g1 = g1_ref[...]
    h = _rms_modulate(x_ref[0], g1, sc1, sh1).astype(BF16)
    has_prev = (i > 0).astype(F32)
    has_next = (i < n_i - 1).astype(F32)
    h_halo = jnp.concatenate([_rms_modulate(xp_ref[0], g1, sc1, sh1) * has_prev,
                              _rms_modulate(xn_ref[0], g1, sc1, sh1) * has_next], axis=0).astype(BF16)

    w_pool_in = win_ref[:, pool0:pool0 + pool_w]
    u_main = jnp.dot(h, w_pool_in, preferred_element_type=F32)
    u_halo = jnp.dot(h_halo, w_pool_in, preferred_element_type=F32)
    qkv = jnp.dot(h, win_ref[:, 0:pool0], preferred_element_type=F32)
    gdim = pool_w // len(POOL_WINDOWS)
    for g in range(len(POOL_WINDOWS)):
        cs = slice(g * gdim, (g + 1) * gdim)
        u_ref[g, 0:HALO, :] = u_halo[0:HALO, cs]
        u_ref[g, HALO:HALO + tt, :] = u_main[:, cs]
        u_ref[g, HALO + tt:, :] = u_halo[HALO:, cs]

    j2 = 2 * lax.broadcasted_iota(jnp.int32, (half, 1), 0)
    for g, w in enumerate(POOL_WINDOWS):
        cs = slice(g * gdim, (g + 1) * gdim)
        halves = []
        for parity in range(2):
            taps = [u_ref[g, pl.ds(HALO + o + parity, half, stride=2), :] for o in range(-w // 2, w // 2)]
            win = taps[0]
            for tap in taps[1:]:
                win = win + tap
            t = i * tt + j2 + parity
            lo = jnp.maximum(t - w // 2, 0)
            hi = jnp.minimum(t + w // 2 - 1, seq_len - 1)
            cnt = (hi - lo + 1).astype(F32)
            halves.append(win / cnt - taps[w // 2])
        pooled = jnp.concatenate(halves, axis=0).astype(BF16)
        mixed = jnp.dot(pooled, wpool_ref[g], preferred_element_type=F32) * pscale_ref[:, cs]
        for parity in range(2):
            y_ref[g, pl.ds(parity, half, stride=2), :] = mixed[parity * half:(parity + 1) * half]
        pool_ref[0, :, cs] = y_ref[g].astype(BF16)

    cos = cos_ref[...]
    sin = sin_ref[...]
    n_col = tt // LANES

    def norm_rope_t(blk, gain_ref):
        xt = blk.T
        gain = pltpu.repeat(gain_ref[...], n_col, axis=1)
        outs = []
        for hh in range(LANES // HEAD_DIM):
            rows = slice(hh * HEAD_DIM, (hh + 1) * HEAD_DIM)
            xh = xt[rows, :]
            ms = jnp.mean(xh * xh, axis=0, keepdims=True)
            y = xh * lax.rsqrt(ms + EPS) * gain[rows, :]
            q4 = HEAD_DIM // 4
            partner = jnp.concatenate([y[q4:2 * q4], y[0:q4], y[3 * q4:], y[2 * q4:3 * q4]], axis=0)
            outs.append(y * cos + partner * sin)
        return outs

    for blk in range(att_w // LANES):
        for hh, qt in enumerate(norm_rope_t(qkv[:, blk * LANES:(blk + 1) * LANES], gq_ref)):
            r0 = blk * LANES + hh * HEAD_DIM
            qT_ref[0, r0:r0 + HEAD_DIM, :] = qt.astype(BF16)

    kt = jnp.concatenate(norm_rope_t(qkv[:, att_w:att_w + kv_w], gk_ref), axis=0)
    k_ref[0] = kt.T.astype(BF16)

    vT_ref[0, 0] = qkv[:, att_w + kv_w:pool0].T.astype(BF16)


def _projection(x, mod, g1, w_in, gq, gk, cos_t, sin_t, w_pool, pool_scale, *, tt, att_w):
    B, T, D = x.shape
    n_t = T // tt
    in_w = w_in.shape[1]
    kv_w = N_KV_HEADS * HEAD_DIM
    pool_w = in_w - att_w - 2 * kv_w
    prev_spec, next_spec = _halo_specs(tt, n_t, D)
    const2 = lambda b, i: (0, 0)
    kern = functools.partial(_proj_kernel, seq_len=T)
    return pl.pallas_call(
        kern,
        grid=(B, n_t),
        in_specs=[pl.BlockSpec((1, tt, D), lambda b, i: (b, i, 0)),
                  prev_spec, next_spec,
                  pl.BlockSpec((1, N_MOD, D), lambda b, i: (b, 0, 0)),
                  pl.BlockSpec((1, D), const2),
                  pl.BlockSpec((D, in_w), const2),
                  pl.BlockSpec((LANES, LANES), const2),
                  pl.BlockSpec((LANES, LANES), const2),
                  pl.BlockSpec((HEAD_DIM, tt), lambda b, i: (0, i)),
                  pl.BlockSpec((HEAD_DIM, tt), lambda b, i: (0, i)),
                  pl.BlockSpec(w_pool.shape, lambda b, i: (0, 0, 0)),
                  pl.BlockSpec((1, pool_w), const2)],
        out_specs=[pl.BlockSpec((1, att_w, tt), lambda b, i: (b, 0, i)),
                   pl.BlockSpec((1, tt, kv_w), lambda b, i: (b, i, 0)),
                   pl.BlockSpec((1, 1, kv_w, tt), lambda b, i: (b, i, 0, 0)),
                   pl.BlockSpec((1, tt, pool_w), lambda b, i: (b, i, 0))],
        out_shape=[jax.ShapeDtypeStruct((B, att_w, T), BF16),
                   jax.ShapeDtypeStruct((B, T, kv_w), BF16),
                   jax.ShapeDtypeStruct((B, n_t, kv_w, tt), BF16),
                   jax.ShapeDtypeStruct((B, T, pool_w), BF16)],
        scratch_shapes=[pltpu.VMEM((len(POOL_WINDOWS), tt + 2 * HALO, LANES), F32),
                        pltpu.VMEM((len(POOL_WINDOWS), tt, LANES), F32)],
        compiler_params=pltpu.CompilerParams(vmem_limit_bytes=VMEM_LIMIT),
        name="proj_qkv_pool",
    )(x, x, x, mod, g1, w_in, gq, gk, cos_t, sin_t, w_pool, pool_scale)


def _attn_kernel(qT_ref, k_ref, vT_ref, pool_ref, x_ref, mod_ref, wout_ref, x1_ref, oT_ref):
    n_chunks = vT_ref.shape[1]
    tk = vT_ref.shape[3]
    att_w = qT_ref.shape[1]
    q_per_kv = att_w // HEAD_DIM // N_KV_HEADS

    n_heads = att_w // HEAD_DIM
    m_run = [None] * n_heads
    l_run = [None] * n_heads

    def scores(c, h):
        g = h // q_per_kv
        ks = k_ref[0, c * tk:(c + 1) * tk, :]
        qh = qT_ref[0, h * HEAD_DIM:(h + 1) * HEAD_DIM, :]
        zero = jnp.zeros_like(qh)
        qpad = jnp.concatenate([qh if gg == g else zero for gg in range(N_KV_HEADS)], axis=0)
        return jnp.dot(ks, qpad, preferred_element_type=F32)

    def accumulate(c, h, s):
        g = h // q_per_kv
        rows = slice(h * HEAD_DIM, (h + 1) * HEAD_DIM)
        vs = vT_ref[0, c, g * HEAD_DIM:(g + 1) * HEAD_DIM, :]
        m_c = jnp.max(s, axis=0, keepdims=True)
        if m_run[h] is None:
            m_new = m_c
            p = jnp.exp2(s - m_new)
            l_run[h] = jnp.sum(p, axis=0, keepdims=True)
            oT_ref[rows, :] = jnp.dot(vs, p.astype(BF16), preferred_element_type=F32)
        else:
            m_new = jnp.maximum(m_run[h], m_c)
            alpha = jnp.exp2(m_run[h] - m_new)
            p = jnp.exp2(s - m_new)
            l_run[h] = alpha * l_run[h] + jnp.sum(p, axis=0, keepdims=True)
            pv = jnp.dot(vs, p.astype(BF16), preferred_element_type=F32)
            oT_ref[rows, :] = alpha * oT_ref[rows, :] + pv
        m_run[h] = m_new

    stages = [(c, g * q_per_kv + hh) for g in range(N_KV_HEADS) for c in range(n_chunks)
              for hh in range(q_per_kv)]
    in_flight = []
    for c, h in stages:
        in_flight.append((c, h, scores(c, h)))
        if len(in_flight) > SCORE_LOOKAHEAD:
            accumulate(*in_flight.pop(0))
    for item in in_flight:
        accumulate(*item)

    for h in range(n_heads):
        rows = slice(h * HEAD_DIM, (h + 1) * HEAD_DIM)
        oT_ref[rows, :] = oT_ref[rows, :] / l_run[h]

    att = oT_ref[...].T.astype(BF16)
    mix = jnp.dot(att, wout_ref[0:att_w, :], preferred_element_type=F32)
    mix = mix + jnp.dot(pool_ref[0], wout_ref[att_w:, :], preferred_element_type=F32)
    g1 = mod_ref[0, 2:3, :]
    x1_ref[0] = x_ref[0] + g1 * mix


def _attention(qT, k, vT, pool, x, mod, w_out, *, tq):
    B, T, D = x.shape
    att_w = qT.shape[1]
    pool_w = pool.shape[2]
    return pl.pallas_call(
        _attn_kernel,
        grid=(B, T // tq),
        in_specs=[pl.BlockSpec((1, att_w, tq), lambda b, i: (b, 0, i)),
                  pl.BlockSpec((1,) + k.shape[1:], lambda b, i: (b, 0, 0)),
                  pl.BlockSpec((1,) + vT.shape[1:], lambda b, i: (b, 0, 0, 0)),
                  pl.BlockSpec((1, tq, pool_w), lambda b, i: (b, i, 0)),
                  pl.BlockSpec((1, tq, D), lambda b, i: (b, i, 0)),
                  pl.BlockSpec((1, N_MOD, D), lambda b, i: (b, 0, 0)),
                  pl.BlockSpec(w_out.shape, lambda b, i: (0, 0))],
        out_specs=pl.BlockSpec((1, tq, D), lambda b, i: (b, i, 0)),
        out_shape=jax.ShapeDtypeStruct((B, T, D), F32),
        scratch_shapes=[pltpu.VMEM((att_w, tq), F32)],
        compiler_params=pltpu.CompilerParams(vmem_limit_bytes=VMEM_LIMIT),
        name="gqa_outproj",
    )(qT, k, vT, pool, x, mod, w_out)


def _ffn_kernel(x_ref, xp_ref, xn_ref, mod_ref, g2_ref, wup_ref, cw_ref, cb_ref, wdn_ref, o_ref,
                h_ref, acc_ref, u_ref, y_ref, *, stage_w):
    i = pl.program_id(1)
    n_i = pl.num_programs(1)
    tf = x_ref.shape[1]
    rows = tf + 2 * HALO
    d_ff = wdn_ref.shape[0]

    sh2 = mod_ref[0, 3:4, :]
    sc2 = mod_ref[0, 4:5, :]
    gate2 = mod_ref[0, 5:6, :]
    g2 = g2_ref[...]
    has_prev = (i > 0).astype(F32)
    has_next = (i < n_i - 1).astype(F32)
    h_ref[...] = jnp.concatenate(
        [_rms_modulate(xp_ref[0], g2, sc2, sh2) * has_prev,
         _rms_modulate(x_ref[0], g2, sc2, sh2),
         _rms_modulate(xn_ref[0], g2, sc2, sh2) * has_next], axis=0).astype(BF16)

    half = tf // 2

    def up(slot, s0, w):
        hv = h_ref[...]
        for part, c0 in enumerate((s0, d_ff + s0)):
            u = jnp.dot(hv, wup_ref[:, c0:c0 + w], preferred_element_type=F32)
            for k in range(w // LANES):
                u_ref[slot, part, k] = u[:, k * LANES:(k + 1) * LANES]

    def conv(slot, part, k, c0, parity):
        cs = slice(c0 + k * LANES, c0 + (k + 1) * LANES)
        taps = [u_ref[slot, part, k, pl.ds(HALO - 1 + parity + j, half, stride=2), :] for j in range(CONV_W)]
        y = taps[0] * cw_ref[0:1, cs] + taps[1] * cw_ref[1:2, cs] + taps[2] * cw_ref[2:3, cs]
        return y + cb_ref[:, cs]

    def down(slot, s0, w):
        blocks = []
        for parity in range(2):
            cols = []
            for k in range(w // LANES):
                gate = conv(slot, 0, k, s0, parity)
                val = conv(slot, 1, k, d_ff + s0, parity)
                cols.append((gate * jax.nn.sigmoid(gate) * val).astype(BF16))
            blocks.append(jnp.concatenate(cols, axis=1))
        a = jnp.concatenate(blocks, axis=0)
        part = jnp.dot(a, wdn_ref[s0:s0 + w, :], preferred_element_type=F32)
        if s0 == 0:
            acc_ref[...] = part
        else:
            acc_ref[...] += part

    stages = [(s0, min(stage_w, d_ff - s0)) for s0 in range(0, d_ff, stage_w)]
    pending = None
    for n, (s0, w) in enumerate(stages):
        up(n % 2, s0, w)
        if pending is not None:
            down(*pending)
        pending = (n % 2, s0, w)
    down(*pending)

    for k in range(acc_ref.shape[1] // LANES):
        cs = slice(k * LANES, (k + 1) * LANES)
        for parity in range(2):
            y_ref[k, pl.ds(parity, half, stride=2), :] = acc_ref[parity * half:(parity + 1) * half, cs]
    for k in range(acc_ref.shape[1] // LANES):
        cs = slice(k * LANES, (k + 1) * LANES)
        o_ref[0, :, cs] = x_ref[0, :, cs] + gate2[:, cs] * y_ref[k]


def _ffn(x1, mod, g2, w_up, conv_w, conv_b, w_down, *, tf, stage_w):
    B, T, D = x1.shape
    n_t = T // tf
    d_ff = w_down.shape[0]
    prev_spec, next_spec = _halo_specs(tf, n_t, D)
    const2 = lambda b, i: (0, 0)
    resident = dict(pipeline_mode=pl.Buffered(1))
    kern = functools.partial(_ffn_kernel, stage_w=stage_w)
    return pl.pallas_call(
        kern,
        grid=(B, n_t),
        in_specs=[pl.BlockSpec((1, tf, D), lambda b, i: (b, i, 0)),
                  prev_spec, next_spec,
                  pl.BlockSpec((1, N_MOD, D), lambda b, i: (b, 0, 0)),
                  pl.BlockSpec((1, D), const2),
                  pl.BlockSpec(w_up.shape, const2, **resident),
                  pl.BlockSpec(conv_w.shape, const2),
                  pl.BlockSpec((1, 2 * d_ff), const2),
                  pl.BlockSpec(w_down.shape, const2, **resident)],
        out_specs=pl.BlockSpec((1, tf, D), lambda b, i: (b, i, 0)),
        out_shape=jax.ShapeDtypeStruct((B, T, D), F32),
        scratch_shapes=[pltpu.VMEM((tf + 2 * HALO, D), BF16),
                        pltpu.VMEM((tf, D), F32),
                        pltpu.VMEM((2, 2, stage_w // LANES, tf + 2 * HALO, LANES), F32),
                        pltpu.VMEM((D // LANES, tf, LANES), F32)],
        compiler_params=pltpu.CompilerParams(vmem_limit_bytes=VMEM_LIMIT),
        name="conv_gated_ffn",
    )(x1, x1, x1, mod, g2, w_up, conv_w, conv_b, w_down)


def _rope_tables(T):
    axis_dim = HEAD_DIM // 2
    pos = jnp.arange(T)
    row = (pos // GRID_W).astype(F32)
    col = (pos % GRID_W).astype(F32)
    d = jnp.arange(HEAD_DIM)
    inv = 1.0 / (ROPE_THETA ** (jnp.arange(0, axis_dim, 2, dtype=F32) / axis_dim))
    freq = inv[d % (axis_dim // 2)]
    p = jnp.where((d // axis_dim == 0)[:, None], row[None, :], col[None, :])
    ang = p * freq[:, None]
    sign = jnp.where((d % axis_dim) < axis_dim // 2, -1.0, 1.0).astype(F32)
    return jnp.cos(ang), jnp.sin(ang) * sign[:, None]


def _head_gain_rows(gain):
    return jnp.broadcast_to(jnp.tile(gain, LANES // HEAD_DIM)[:, None], (LANES, LANES))


def kernel(x, c, w_ada, b_ada, norm1_g, w_in, q_norm_g, k_norm_g, w_pool, pool_scale, w_out, norm2_g,
           w_up, conv_w, conv_b, w_down):
    B, T, D = x.shape
    depth = w_ada.shape[0]
    att_w = D // 2
    tile = 512
    assert T % tile == 0 and T % GRID_W == 0 and D % LANES == 0
    d_ff = w_down.shape[1]
    q_scale = math.log2(math.e) / math.sqrt(HEAD_DIM)
    cos_t, sin_t = _rope_tables(T)

    for l in range(depth):
        mod = _modulation(c, w_ada[l], b_ada[l]).reshape(B, N_MOD, D)
        qT, k, vT, pool = _projection(
            x, mod, norm1_g[l].reshape(1, D), w_in[l].astype(BF16),
            _head_gain_rows(q_norm_g[l] * q_scale), _head_gain_rows(k_norm_g[l]),
            cos_t, sin_t, w_pool[l].astype(BF16), pool_scale[l].reshape(1, -1),
            tt=tile, att_w=att_w)
        x1 = _attention(qT, k, vT, pool, x, mod, w_out[l].astype(BF16), tq=tile)
        x = _ffn(x1, mod, norm2_g[l].reshape(1, D), w_up[l].astype(BF16), conv_w[l],
                 conv_b[l].reshape(1, -1), w_down[l].astype(BF16), tf=tile, stage_w=512)
    return x
```

```python
import functools
import math

import jax
import jax.numpy as jnp
from jax import lax
from jax.experimental import pallas as pl
from jax.experimental.pallas import tpu as pltpu

HEAD_DIM = 64
N_KV_HEADS = 2
GRID_W = 64
ROPE_THETA = 10000.0
POOL_WINDOWS = (2, 4, 8, 16)
CONV_W = 3
N_MOD = 6
EPS = 1e-6

LANES = 128
HALO = 8
VMEM_LIMIT = 56 * 1024 * 1024
SCORE_LOOKAHEAD = 2
SCORE_BOUND = 64.0
ONES_ROWS = 16

F32 = jnp.float32
BF16 = jnp.bfloat16


def _rms_modulate(xv, g, sc, sh):
    ms = jnp.mean(xv * xv, axis=-1, keepdims=True)
    return (xv * lax.rsqrt(ms + EPS) * g) * (1.0 + sc) + sh


def _halo_specs(tile, n_tiles, width):
    per = tile // HALO
    last = n_tiles * per - 1
    prev = pl.BlockSpec((1, HALO, width), lambda b, i: (b, jnp.maximum(i * per - 1, 0), 0))
    nxt = pl.BlockSpec((1, HALO, width), lambda b, i: (b, jnp.minimum((i + 1) * per, last), 0))
    return prev, nxt


def _mod_kernel(c_ref, w_ref, b_ref, o_ref):
    c = c_ref[...]
    c_act = c * jax.nn.sigmoid(c)
    o_ref[...] = jnp.dot(c_act, w_ref[...], preferred_element_type=F32) + b_ref[...]


def _modulation(c, w_ada, b_ada):
    B, D = c.shape
    n = w_ada.shape[1]
    bn = D
    return pl.pallas_call(
        _mod_kernel,
        grid=(n // bn,),
        in_specs=[pl.BlockSpec((B, D), lambda j: (0, 0)),
                  pl.BlockSpec((D, bn), lambda j: (0, j)),
                  pl.BlockSpec((1, bn), lambda j: (0, j))],
        out_specs=pl.BlockSpec((B, bn), lambda j: (0, j)),
        out_shape=jax.ShapeDtypeStruct((B, n), F32),
        compiler_params=pltpu.CompilerParams(vmem_limit_bytes=VMEM_LIMIT),
        name="adaln_modulation",
    )(c, w_ada, b_ada.reshape(1, n))


def _proj_kernel(x_ref, xp_ref, xn_ref, mod_ref, g1_ref, win_ref, gq_ref, gk_ref, cos_ref, sin_ref,
                 wpool_ref, pscale_ref, qT_ref, k_ref, vT_ref, pool_ref, qn_ref, kn_ref, u_ref, y_ref, *, seq_len):
    i = pl.program_id(1)
    n_i = pl.num_programs(1)
    tt = x_ref.shape[1]
    half = tt // 2
    att_w = qT_ref.shape[1]
    kv_w = N_KV_HEADS * HEAD_DIM
    pool_w = pool_ref.shape[2]
    pool0 = att_w + 2 * kv_w

    sh1 = mod_ref[0, 0:1, :]
    sc1 = mod_ref[0, 1:2, :]
    g1 = g1_ref[...]
    h = _rms_modulate(x_ref[0], g1, sc1, sh1).astype(BF16)
    has_prev = (i > 0).astype(F32)
    has_next = (i < n_i - 1).astype(F32)
    h_halo = jnp.concatenate([_rms_modulate(xp_ref[0], g1, sc1, sh1) * has_prev,
                              _rms_modulate(xn_ref[0], g1, sc1, sh1) * has_next], axis=0).astype(BF16)

    w_pool_in = win_ref[:, pool0:pool0 + pool_w]
    u_main = jnp.dot(h, w_pool_in, preferred_element_type=F32)
    u_halo = jnp.dot(h_halo, w_pool_in, preferred_element_type=F32)
    qkv = jnp.dot(h, win_ref[:, 0:pool0], preferred_element_type=F32)
    gdim = pool_w // len(POOL_WINDOWS)
    for g in range(len(POOL_WINDOWS)):
        cs = slice(g * gdim, (g + 1) * gdim)
        u_ref[g, 0:HALO, :] = u_halo[0:HALO, cs]
        u_ref[g, HALO:HALO + tt, :] = u_main[:, cs]
        u_ref[g, HALO + tt:, :] = u_halo[HALO:, cs]

    j2 = 2 * lax.broadcasted_iota(jnp.int32, (half, 1), 0)
    for g, w in enumerate(POOL_WINDOWS):
        cs = slice(g * gdim, (g + 1) * gdim)
        halves = []
        for parity in range(2):
            taps = [u_ref[g, pl.ds(HALO + o + parity, half, stride=2), :] for o in range(-w // 2, w // 2)]
            win = taps[0]
            for tap in taps[1:]:
                win = win + tap
            t = i * tt + j2 + parity
            lo = jnp.maximum(t - w // 2, 0)
            hi = jnp.minimum(t + w // 2 - 1, seq_len - 1)
            cnt = (hi - lo + 1).astype(F32)
            halves.append(win / cnt - taps[w // 2])
        pooled = jnp.concatenate(halves, axis=0).astype(BF16)
        mixed = jnp.dot(pooled, wpool_ref[g], preferred_element_type=F32) * pscale_ref[:, cs]
        for parity in range(2):
            y_ref[g, pl.ds(parity, half, stride=2), :] = mixed[parity * half:(parity + 1) * half]
        pool_ref[0, :, cs] = y_ref[g].astype(BF16)

    cos = cos_ref[...]
    sin = sin_ref[...]
    n_col = tt // LANES

    def norm_rope_t(blk, gain_ref):
        xt = blk.T
        gain = jnp.concatenate([gain_ref[...]] * n_col, axis=1)
        outs = []
        for hh in range(LANES // HEAD_DIM):
            rows = slice(hh * HEAD_DIM, (hh + 1) * HEAD_DIM)
            xh = xt[rows, :]
            ms = jnp.mean(xh * xh, axis=0, keepdims=True)
            y = xh * lax.rsqrt(ms + EPS) * gain[rows, :]
            q4 = HEAD_DIM // 4
            partner = jnp.concatenate([y[q4:2 * q4], y[0:q4], y[3 * q4:], y[2 * q4:3 * q4]], axis=0)
            outs.append(y * cos + partner * sin)
        return outs

    def sq_norm(rows_t):
        return jnp.sum(rows_t * rows_t, axis=0, keepdims=True)

    for blk in range(att_w // LANES):
        for hh, qt in enumerate(norm_rope_t(qkv[:, blk * LANES:(blk + 1) * LANES], gq_ref)):
            h = blk * (LANES // HEAD_DIM) + hh
            qT_ref[0, h * HEAD_DIM:(h + 1) * HEAD_DIM, :] = qt.astype(BF16)
            qn_ref[0, h:h + 1, :] = sq_norm(qt)

    k_heads = norm_rope_t(qkv[:, att_w:att_w + kv_w], gk_ref)
    for g, kt in enumerate(k_heads):
        kn_ref[0, g:g + 1, :] = sq_norm(kt)
    k_ref[0] = jnp.concatenate(k_heads, axis=0).T.astype(BF16)

    vT_ref[0, 0] = qkv[:, att_w + kv_w:pool0].T.astype(BF16)


def _projection(x, mod, g1, w_in, gq, gk, cos_t, sin_t, w_pool, pool_scale, *, tt, att_w):
    B, T, D = x.shape
    n_t = T // tt
    in_w = w_in.shape[1]
    kv_w = N_KV_HEADS * HEAD_DIM
    pool_w = in_w - att_w - 2 * kv_w
    prev_spec, next_spec = _halo_specs(tt, n_t, D)
    const2 = lambda b, i: (0, 0)
    kern = functools.partial(_proj_kernel, seq_len=T)
    return pl.pallas_call(
        kern,
        grid=(B, n_t),
        in_specs=[pl.BlockSpec((1, tt, D), lambda b, i: (b, i, 0)),
                  prev_spec, next_spec,
                  pl.BlockSpec((1, N_MOD, D), lambda b, i: (b, 0, 0)),
                  pl.BlockSpec((1, D), const2),
                  pl.BlockSpec((D, in_w), const2),
                  pl.BlockSpec((LANES, LANES), const2),
                  pl.BlockSpec((LANES, LANES), const2),
                  pl.BlockSpec((HEAD_DIM, tt), lambda b, i: (0, i)),
                  pl.BlockSpec((HEAD_DIM, tt), lambda b, i: (0, i)),
                  pl.BlockSpec(w_pool.shape, lambda b, i: (0, 0, 0)),
                  pl.BlockSpec((1, pool_w), const2)],
        out_specs=[pl.BlockSpec((1, att_w, tt), lambda b, i: (b, 0, i)),
                   pl.BlockSpec((1, tt, kv_w), lambda b, i: (b, i, 0)),
                   pl.BlockSpec((1, 1, kv_w, tt), lambda b, i: (b, i, 0, 0)),
                   pl.BlockSpec((1, tt, pool_w), lambda b, i: (b, i, 0)),
                   pl.BlockSpec((1, att_w // HEAD_DIM, tt), lambda b, i: (b, 0, i)),
                   pl.BlockSpec((1, N_KV_HEADS, tt), lambda b, i: (b, 0, i))],
        out_shape=[jax.ShapeDtypeStruct((B, att_w, T), BF16),
                   jax.ShapeDtypeStruct((B, T, kv_w), BF16),
                   jax.ShapeDtypeStruct((B, n_t, kv_w, tt), BF16),
                   jax.ShapeDtypeStruct((B, T, pool_w), BF16),
                   jax.ShapeDtypeStruct((B, att_w // HEAD_DIM, T), F32),
                   jax.ShapeDtypeStruct((B, N_KV_HEADS, T), F32)],
        scratch_shapes=[pltpu.VMEM((len(POOL_WINDOWS), tt + 2 * HALO, LANES), F32),
                        pltpu.VMEM((len(POOL_WINDOWS), tt, LANES), F32)],
        compiler_params=pltpu.CompilerParams(vmem_limit_bytes=VMEM_LIMIT),
        name="proj_qkv_pool",
    )(x, x, x, mod, g1, w_in, gq, gk, cos_t, sin_t, w_pool, pool_scale)


def _attn_kernel(qT_ref, k_ref, vT_ref, pool_ref, x_ref, mod_ref, wout_ref, qn_ref, kn_ref, x1_ref,
                 oT_ref, acc_ref):
    n_chunks = vT_ref.shape[1]
    tk = vT_ref.shape[3]
    tq = qT_ref.shape[2]
    att_w = qT_ref.shape[1]
    n_heads = att_w // HEAD_DIM
    q_per_kv = n_heads // N_KV_HEADS

    def head_rows(h):
        return slice(h * HEAD_DIM, (h + 1) * HEAD_DIM)

    def padded_q(qh, g):
        zero = jnp.zeros_like(qh)
        return jnp.concatenate([qh if gg == g else zero for gg in range(N_KV_HEADS)], axis=0)

    bound_sq = jnp.max(qn_ref[0]) * jnp.max(kn_ref[0])
    bounded = bound_sq <= SCORE_BOUND * SCORE_BOUND

    @pl.when(bounded)
    def _():
        ones = jnp.ones((ONES_ROWS, tk), BF16)

        def scores(c, h):
            ks = k_ref[0, c * tk:(c + 1) * tk, :]
            return jnp.dot(ks, padded_q(qT_ref[0, head_rows(h), :], h // q_per_kv),
                           preferred_element_type=F32)

        def accumulate(c, h, s):
            g = h // q_per_kv
            v_aug = jnp.concatenate([vT_ref[0, c, head_rows(g), :], ones], axis=0)
            pv = jnp.dot(v_aug, jnp.exp2(s).astype(BF16), preferred_element_type=F32)
            if c == 0:
                acc_ref[h] = pv
            else:
                acc_ref[h] += pv

        stages = [(c, g * q_per_kv + hh) for g in range(N_KV_HEADS) for c in range(n_chunks)
                  for hh in range(q_per_kv)]
        in_flight = []
        for c, h in stages:
            in_flight.append((c, h, scores(c, h)))
            if len(in_flight) > SCORE_LOOKAHEAD:
                accumulate(*in_flight.pop(0))
        for item in in_flight:
            accumulate(*item)
        for h in range(n_heads):
            oT_ref[head_rows(h), :] = acc_ref[h, 0:HEAD_DIM, :] / acc_ref[h, HEAD_DIM:HEAD_DIM + 1, :]

    @pl.when(jnp.logical_not(bounded))
    def _():
        def per_head(h, carry):
            g = h // q_per_kv
            qh = qT_ref[0, pl.ds(pl.multiple_of(h * HEAD_DIM, HEAD_DIM), HEAD_DIM), :]
            zero = jnp.zeros_like(qh)
            qpad = jnp.concatenate([jnp.where(g == gg, qh, zero) for gg in range(N_KV_HEADS)], axis=0)

            def chunk(c, state):
                m, l, acc = state
                ks = k_ref[0, pl.ds(pl.multiple_of(c * tk, tk), tk), :]
                vs = vT_ref[0, c, pl.ds(pl.multiple_of(g * HEAD_DIM, HEAD_DIM), HEAD_DIM), :]
                s = jnp.dot(ks, qpad, preferred_element_type=F32)
                m_new = jnp.maximum(m, jnp.max(s, axis=0, keepdims=True))
                alpha = jnp.exp2(m - m_new)
                p = jnp.exp2(s - m_new)
                l = alpha * l + jnp.sum(p, axis=0, keepdims=True)
                acc = alpha * acc + jnp.dot(vs, p.astype(BF16), preferred_element_type=F32)
                return m_new, l, acc

            init = (jnp.full((1, tq), -jnp.inf, F32), jnp.zeros((1, tq), F32), jnp.zeros((HEAD_DIM, tq), F32))
            _, l, acc = lax.fori_loop(0, n_chunks, chunk, init)
            oT_ref[pl.ds(pl.multiple_of(h * HEAD_DIM, HEAD_DIM), HEAD_DIM), :] = acc / l
            return carry

        lax.fori_loop(0, n_heads, per_head, 0)

    att = oT_ref[...].T.astype(BF16)
    mix = jnp.dot(att, wout_ref[0:att_w, :], preferred_element_type=F32)
    mix = mix + jnp.dot(pool_ref[0], wout_ref[att_w:, :], preferred_element_type=F32)
    g1 = mod_ref[0, 2:3, :]
    x1_ref[0] = x_ref[0] + g1 * mix


def _attention(qT, k, vT, pool, x, mod, w_out, qn, kn, *, tq):
    B, T, D = x.shape
    att_w = qT.shape[1]
    pool_w = pool.shape[2]
    return pl.pallas_call(
        _attn_kernel,
        grid=(B, T // tq),
        in_specs=[pl.BlockSpec((1, att_w, tq), lambda b, i: (b, 0, i)),
                  pl.BlockSpec((1,) + k.shape[1:], lambda b, i: (b, 0, 0)),
                  pl.BlockSpec((1,) + vT.shape[1:], lambda b, i: (b, 0, 0, 0)),
                  pl.BlockSpec((1, tq, pool_w), lambda b, i: (b, i, 0)),
                  pl.BlockSpec((1, tq, D), lambda b, i: (b, i, 0)),
                  pl.BlockSpec((1, N_MOD, D), lambda b, i: (b, 0, 0)),
                  pl.BlockSpec(w_out.shape, lambda b, i: (0, 0)),
                  pl.BlockSpec((1, qn.shape[1], tq), lambda b, i: (b, 0, i)),
                  pl.BlockSpec((1,) + kn.shape[1:], lambda b, i: (b, 0, 0))],
        out_specs=pl.BlockSpec((1, tq, D), lambda b, i: (b, i, 0)),
        out_shape=jax.ShapeDtypeStruct((B, T, D), F32),
        scratch_shapes=[pltpu.VMEM((att_w, tq), F32),
                        pltpu.VMEM((att_w // HEAD_DIM, HEAD_DIM + ONES_ROWS, tq), F32)],
        compiler_params=pltpu.CompilerParams(vmem_limit_bytes=VMEM_LIMIT),
        name="gqa_outproj",
    )(qT, k, vT, pool, x, mod, w_out, qn, kn)


def _ffn_kernel(x_ref, xp_ref, xn_ref, mod_ref, g2_ref, wup_ref, cw_ref, cb_ref, wdn_ref, o_ref,
                h_ref, acc_ref, u_ref, y_ref, *, stage_w):
    i = pl.program_id(1)
    n_i = pl.num_programs(1)
    tf = x_ref.shape[1]
    rows = tf + 2 * HALO
    d_ff = wdn_ref.shape[0]

    sh2 = mod_ref[0, 3:4, :]
    sc2 = mod_ref[0, 4:5, :]
    gate2 = mod_ref[0, 5:6, :]
    g2 = g2_ref[...]
    has_prev = (i > 0).astype(F32)
    has_next = (i < n_i - 1).astype(F32)
    h_ref[...] = jnp.concatenate(
        [_rms_modulate(xp_ref[0], g2, sc2, sh2) * has_prev,
         _rms_modulate(x_ref[0], g2, sc2, sh2),
         _rms_modulate(xn_ref[0], g2, sc2, sh2) * has_next], axis=0).astype(BF16)

    half = tf // 2

    def up(slot, s0, w):
        hv = h_ref[...]
        for part, c0 in enumerate((s0, d_ff + s0)):
            u = jnp.dot(hv, wup_ref[:, c0:c0 + w], preferred_element_type=F32)
            for k in range(w // LANES):
                u_ref[slot, part, k] = u[:, k * LANES:(k + 1) * LANES]

    def conv(slot, part, k, c0, parity):
        cs = slice(c0 + k * LANES, c0 + (k + 1) * LANES)
        taps = [u_ref[slot, part, k, pl.ds(HALO - 1 + parity + j, half, stride=2), :] for j in range(CONV_W)]
        y = taps[0] * cw_ref[0:1, cs] + taps[1] * cw_ref[1:2, cs] + taps[2] * cw_ref[2:3, cs]
        return y + cb_ref[:, cs]

    def down(slot, s0, w):
        blocks = []
        for parity in range(2):
            cols = []
            for k in range(w // LANES):
                gate = conv(slot, 0, k, s0, parity)
                val = conv(slot, 1, k, d_ff + s0, parity)
                cols.append((gate * jax.nn.sigmoid(gate) * val).astype(BF16))
            blocks.append(jnp.concatenate(cols, axis=1))
        a = jnp.concatenate(blocks, axis=0)
        part = jnp.dot(a, wdn_ref[s0:s0 + w, :], preferred_element_type=F32)
        if s0 == 0:
            acc_ref[...] = part
        else:
            acc_ref[...] += part

    stages = [(s0, min(stage_w, d_ff - s0)) for s0 in range(0, d_ff, stage_w)]
    pending = None
    for n, (s0, w) in enumerate(stages):
        up(n % 2, s0, w)
        if pending is not None:
            down(*pending)
        pending = (n % 2, s0, w)
    down(*pending)

    for k in range(acc_ref.shape[1] // LANES):
        cs = slice(k * LANES, (k + 1) * LANES)
        for parity in range(2):
            y_ref[k, pl.ds(parity, half, stride=2), :] = acc_ref[parity * half:(parity + 1) * half, cs]
    for k in range(acc_ref.shape[1] // LANES):
        cs = slice(k * LANES, (k + 1) * LANES)
        o_ref[0, :, cs] = x_ref[0, :, cs] + gate2[:, cs] * y_ref[k]


def _ffn(x1, mod, g2, w_up, conv_w, conv_b, w_down, *, tf, stage_w):
    B, T, D = x1.shape
    n_t = T // tf
    d_ff = w_down.shape[0]
    prev_spec, next_spec = _halo_specs(tf, n_t, D)
    const2 = lambda b, i: (0, 0)
    resident = dict(pipeline_mode=pl.Buffered(1))
    kern = functools.partial(_ffn_kernel, stage_w=stage_w)
    return pl.pallas_call(
        kern,
        grid=(B, n_t),
        in_specs=[pl.BlockSpec((1, tf, D), lambda b, i: (b, i, 0)),
                  prev_spec, next_spec,
                  pl.BlockSpec((1, N_MOD, D), lambda b, i: (b, 0, 0)),
                  pl.BlockSpec((1, D), const2),
                  pl.BlockSpec(w_up.shape, const2, **resident),
                  pl.BlockSpec(conv_w.shape, const2),
                  pl.BlockSpec((1, 2 * d_ff), const2),
                  pl.BlockSpec(w_down.shape, const2, **resident)],
        out_specs=pl.BlockSpec((1, tf, D), lambda b, i: (b, i, 0)),
        out_shape=jax.ShapeDtypeStruct((B, T, D), F32),
        scratch_shapes=[pltpu.VMEM((tf + 2 * HALO, D), BF16),
                        pltpu.VMEM((tf, D), F32),
                        pltpu.VMEM((2, 2, stage_w // LANES, tf + 2 * HALO, LANES), F32),
                        pltpu.VMEM((D // LANES, tf, LANES), F32)],
        compiler_params=pltpu.CompilerParams(vmem_limit_bytes=VMEM_LIMIT),
        name="conv_gated_ffn",
    )(x1, x1, x1, mod, g2, w_up, conv_w, conv_b, w_down)


def _rope_tables(T):
    axis_dim = HEAD_DIM // 2
    pos = jnp.arange(T)
    row = (pos // GRID_W).astype(F32)
    col = (pos % GRID_W).astype(F32)
    d = jnp.arange(HEAD_DIM)
    inv = 1.0 / (ROPE_THETA ** (jnp.arange(0, axis_dim, 2, dtype=F32) / axis_dim))
    freq = inv[d % (axis_dim // 2)]
    p = jnp.where((d // axis_dim == 0)[:, None], row[None, :], col[None, :])
    ang = p * freq[:, None]
    sign = jnp.where((d % axis_dim) < axis_dim // 2, -1.0, 1.0).astype(F32)
    return jnp.cos(ang), jnp.sin(ang) * sign[:, None]


def _head_gain_rows(gain):
    return jnp.broadcast_to(jnp.tile(gain, LANES // HEAD_DIM)[:, None], (LANES, LANES))


def kernel(x, c, w_ada, b_ada, norm1_g, w_in, q_norm_g, k_norm_g, w_pool, pool_scale, w_out, norm2_g,
           w_up, conv_w, conv_b, w_down):
    B, T, D = x.shape
    depth = w_ada.shape[0]
    att_w = D // 2
    tile = 512
    assert T % tile == 0 and T % GRID_W == 0 and D % LANES == 0
    d_ff = w_down.shape[1]
    q_scale = math.log2(math.e) / math.sqrt(HEAD_DIM)
    cos_t, sin_t = _rope_tables(T)

    for l in range(depth):
        mod = _modulation(c, w_ada[l], b_ada[l]).reshape(B, N_MOD, D)
        qT, k, vT, pool, qn, kn = _projection(
            x, mod, norm1_g[l].reshape(1, D), w_in[l].astype(BF16),
            _head_gain_rows(q_norm_g[l] * q_scale), _head_gain_rows(k_norm_g[l]),
            cos_t, sin_t, w_pool[l].astype(BF16), pool_scale[l].reshape(1, -1),
            tt=tile, att_w=att_w)
        x1 = _attention(qT, k, vT, pool, x, mod, w_out[l].astype(BF16), qn, kn, tq=tile)
        x = _ffn(x1, mod, norm2_g[l].reshape(1, D), w_up[l].astype(BF16), conv_w[l],
                 conv_b[l].reshape(1, -1), w_down[l].astype(BF16), tf=tile, stage_w=512)
    return x
```

```python
import functools
import math

import jax
import jax.numpy as jnp
from jax import lax
from jax.experimental import pallas as pl
from jax.experimental.pallas import tpu as pltpu

HEAD_DIM = 64
N_KV_HEADS = 2
GRID_W = 64
ROPE_THETA = 10000.0
POOL_WINDOWS = (2, 4, 8, 16)
CONV_W = 3
N_MOD = 6
EPS = 1e-6

LANES = 128
HALO = 8
VMEM_LIMIT = 56 * 1024 * 1024
SCORE_LOOKAHEAD = 1
SCORE_BOUND = 64.0
ONES_ROWS = 16

F32 = jnp.float32
BF16 = jnp.bfloat16


def _rms_modulate(xv, g, sc, sh):
    ms = jnp.mean(xv * xv, axis=-1, keepdims=True)
    return (xv * lax.rsqrt(ms + EPS) * g) * (1.0 + sc) + sh


def _halo_specs(tile, n_tiles, width):
    per = tile // HALO
    last = n_tiles * per - 1
    prev = pl.BlockSpec((1, HALO, width), lambda b, i: (b, jnp.maximum(i * per - 1, 0), 0))
    nxt = pl.BlockSpec((1, HALO, width), lambda b, i: (b, jnp.minimum((i + 1) * per, last), 0))
    return prev, nxt


def _mod_kernel(c_ref, w_ref, b_ref, o_ref):
    c = c_ref[...]
    c_act = c * jax.nn.sigmoid(c)
    o_ref[...] = jnp.dot(c_act, w_ref[...], preferred_element_type=F32) + b_ref[...]


def _modulation(c, w_ada, b_ada):
    B, D = c.shape
    n = w_ada.shape[1]
    bn = D
    return pl.pallas_call(
        _mod_kernel,
        grid=(n // bn,),
        in_specs=[pl.BlockSpec((B, D), lambda j: (0, 0)),
                  pl.BlockSpec((D, bn), lambda j: (0, j)),
                  pl.BlockSpec((1, bn), lambda j: (0, j))],
        out_specs=pl.BlockSpec((B, bn), lambda j: (0, j)),
        out_shape=jax.ShapeDtypeStruct((B, n), F32),
        compiler_params=pltpu.CompilerParams(vmem_limit_bytes=VMEM_LIMIT),
        name="adaln_modulation",
    )(c, w_ada, b_ada.reshape(1, n))


def _proj_kernel(x_ref, xp_ref, xn_ref, mod_ref, g1_ref, win_ref, gq_ref, gk_ref, cos_ref, sin_ref,
                 wpool_ref, pscale_ref, qT_ref, k_ref, vT_ref, pool_ref, qn_ref, kn_ref, u_ref, y_ref, *, seq_len):
    i = pl.program_id(1)
    n_i = pl.num_programs(1)
    tt = x_ref.shape[1]
    half = tt // 2
    att_w = qT_ref.shape[1]
    kv_w = N_KV_HEADS * HEAD_DIM
    pool_w = pool_ref.shape[2]
    pool0 = att_w + 2 * kv_w

    sh1 = mod_ref[0, 0:1, :]
    sc1 = mod_ref[0, 1:2, :]
    g1 = g1_ref[...]
    h = _rms_modulate(x_ref[0], g1, sc1, sh1).astype(BF16)
    has_prev = (i > 0).astype(F32)
    has_next = (i < n_i - 1).astype(F32)
    h_halo = jnp.concatenate([_rms_modulate(xp_ref[0], g1, sc1, sh1) * has_prev,
                              _rms_modulate(xn_ref[0], g1, sc1, sh1) * has_next], axis=0).astype(BF16)

    w_pool_in = win_ref[:, pool0:pool0 + pool_w]
    u_main = jnp.dot(h, w_pool_in, preferred_element_type=F32)
    u_halo = jnp.dot(h_halo, w_pool_in, preferred_element_type=F32)
    qkv = jnp.dot(h, win_ref[:, 0:pool0], preferred_element_type=F32)
    gdim = pool_w // len(POOL_WINDOWS)
    for g in range(len(POOL_WINDOWS)):
        cs = slice(g * gdim, (g + 1) * gdim)
        u_ref[g, 0:HALO, :] = u_halo[0:HALO, cs]
        u_ref[g, HALO:HALO + tt, :] = u_main[:, cs]
        u_ref[g, HALO + tt:, :] = u_halo[HALO:, cs]

    j2 = 2 * lax.broadcasted_iota(jnp.int32, (half, 1), 0)
    for g, w in enumerate(POOL_WINDOWS):
        cs = slice(g * gdim, (g + 1) * gdim)
        halves = []
        for parity in range(2):
            taps = [u_ref[g, pl.ds(HALO + o + parity, half, stride=2), :] for o in range(-w // 2, w // 2)]
            win = taps[0]
            for tap in taps[1:]:
                win = win + tap
            t = i * tt + j2 + parity
            lo = jnp.maximum(t - w // 2, 0)
            hi = jnp.minimum(t + w // 2 - 1, seq_len - 1)
            cnt = (hi - lo + 1).astype(F32)
            halves.append(win / cnt - taps[w // 2])
        pooled = jnp.concatenate(halves, axis=0).astype(BF16)
        mixed = jnp.dot(pooled, wpool_ref[g], preferred_element_type=F32) * pscale_ref[:, cs]
        for parity in range(2):
            y_ref[g, pl.ds(parity, half, stride=2), :] = mixed[parity * half:(parity + 1) * half]
        pool_ref[0, :, cs] = y_ref[g].astype(BF16)

    cos = cos_ref[...]
    sin = sin_ref[...]
    n_col = tt // LANES

    def norm_rope_t(blk, gain_ref):
        xt = blk.T
        gain = jnp.concatenate([gain_ref[...]] * n_col, axis=1)
        outs = []
        for hh in range(LANES // HEAD_DIM):
            rows = slice(hh * HEAD_DIM, (hh + 1) * HEAD_DIM)
            xh = xt[rows, :]
            ms = jnp.mean(xh * xh, axis=0, keepdims=True)
            y = xh * lax.rsqrt(ms + EPS) * gain[rows, :]
            q4 = HEAD_DIM // 4
            partner = jnp.concatenate([y[q4:2 * q4], y[0:q4], y[3 * q4:], y[2 * q4:3 * q4]], axis=0)
            outs.append(y * cos + partner * sin)
        return outs

    def sq_norm(rows_t):
        return jnp.sum(rows_t * rows_t, axis=0, keepdims=True)

    for blk in range(att_w // LANES):
        for hh, qt in enumerate(norm_rope_t(qkv[:, blk * LANES:(blk + 1) * LANES], gq_ref)):
            h = blk * (LANES // HEAD_DIM) + hh
            qT_ref[0, h * HEAD_DIM:(h + 1) * HEAD_DIM, :] = qt.astype(BF16)
            qn_ref[0, h:h + 1, :] = sq_norm(qt)

    k_heads = norm_rope_t(qkv[:, att_w:att_w + kv_w], gk_ref)
    for g, kt in enumerate(k_heads):
        kn_ref[0, g:g + 1, :] = sq_norm(kt)
    k_ref[0] = jnp.concatenate(k_heads, axis=0).T.astype(BF16)

    vT_ref[0, 0] = qkv[:, att_w + kv_w:pool0].T.astype(BF16)


def _projection(x, mod, g1, w_in, gq, gk, cos_t, sin_t, w_pool, pool_scale, *, tt, att_w):
    B, T, D = x.shape
    n_t = T // tt
    in_w = w_in.shape[1]
    kv_w = N_KV_HEADS * HEAD_DIM
    pool_w = in_w - att_w - 2 * kv_w
    prev_spec, next_spec = _halo_specs(tt, n_t, D)
    const2 = lambda b, i: (0, 0)
    kern = functools.partial(_proj_kernel, seq_len=T)
    return pl.pallas_call(
        kern,
        grid=(B, n_t),
        in_specs=[pl.BlockSpec((1, tt, D), lambda b, i: (b, i, 0)),
                  prev_spec, next_spec,
                  pl.BlockSpec((1, N_MOD, D), lambda b, i: (b, 0, 0)),
                  pl.BlockSpec((1, D), const2),
                  pl.BlockSpec((D, in_w), const2),
                  pl.BlockSpec((LANES, LANES), const2),
                  pl.BlockSpec((LANES, LANES), const2),
                  pl.BlockSpec((HEAD_DIM, tt), lambda b, i: (0, i)),
                  pl.BlockSpec((HEAD_DIM, tt), lambda b, i: (0, i)),
                  pl.BlockSpec(w_pool.shape, lambda b, i: (0, 0, 0)),
                  pl.BlockSpec((1, pool_w), const2)],
        out_specs=[pl.BlockSpec((1, att_w, tt), lambda b, i: (b, 0, i)),
                   pl.BlockSpec((1, tt, kv_w), lambda b, i: (b, i, 0)),
                   pl.BlockSpec((1, 1, kv_w, tt), lambda b, i: (b, i, 0, 0)),
                   pl.BlockSpec((1, tt, pool_w), lambda b, i: (b, i, 0)),
                   pl.BlockSpec((1, att_w // HEAD_DIM, tt), lambda b, i: (b, 0, i)),
                   pl.BlockSpec((1, N_KV_HEADS, tt), lambda b, i: (b, 0, i))],
        out_shape=[jax.ShapeDtypeStruct((B, att_w, T), BF16),
                   jax.ShapeDtypeStruct((B, T, kv_w), BF16),
                   jax.ShapeDtypeStruct((B, n_t, kv_w, tt), BF16),
                   jax.ShapeDtypeStruct((B, T, pool_w), BF16),
                   jax.ShapeDtypeStruct((B, att_w // HEAD_DIM, T), F32),
                   jax.ShapeDtypeStruct((B, N_KV_HEADS, T), F32)],
        scratch_shapes=[pltpu.VMEM((len(POOL_WINDOWS), tt + 2 * HALO, LANES), F32),
                        pltpu.VMEM((len(POOL_WINDOWS), tt, LANES), F32)],
        compiler_params=pltpu.CompilerParams(vmem_limit_bytes=VMEM_LIMIT),
        name="proj_qkv_pool",
    )(x, x, x, mod, g1, w_in, gq, gk, cos_t, sin_t, w_pool, pool_scale)


def _attn_kernel(qT_ref, k_ref, vT_ref, pool_ref, x_ref, mod_ref, wout_ref, qn_ref, kn_ref, x1_ref,
                 oT_ref, acc_ref):
    n_chunks = vT_ref.shape[1]
    tk = vT_ref.shape[3]
    tq = qT_ref.shape[2]
    att_w = qT_ref.shape[1]
    n_heads = att_w // HEAD_DIM
    q_per_kv = n_heads // N_KV_HEADS

    def head_rows(h):
        return slice(h * HEAD_DIM, (h + 1) * HEAD_DIM)

    def padded_q(qh, g):
        zero = jnp.zeros_like(qh)
        return jnp.concatenate([qh if gg == g else zero for gg in range(N_KV_HEADS)], axis=0)

    bound_sq = jnp.max(qn_ref[0]) * jnp.max(kn_ref[0])
    bounded = bound_sq <= SCORE_BOUND * SCORE_BOUND

    @pl.when(bounded)
    def _():
        ones = jnp.ones((ONES_ROWS, tk), BF16)

        def scores(c, h):
            ks = k_ref[0, c * tk:(c + 1) * tk, :]
            return jnp.dot(ks, padded_q(qT_ref[0, head_rows(h), :], h // q_per_kv),
                           preferred_element_type=F32)

        def accumulate(c, h, s):
            g = h // q_per_kv
            v_aug = jnp.concatenate([vT_ref[0, c, head_rows(g), :], ones], axis=0)
            pv = jnp.dot(v_aug, jnp.exp2(s).astype(BF16), preferred_element_type=F32)
            if c == 0:
                acc_ref[h] = pv
            else:
                acc_ref[h] += pv

        stages = [(c, g * q_per_kv + hh) for g in range(N_KV_HEADS) for c in range(n_chunks)
                  for hh in range(q_per_kv)]
        in_flight = []
        for c, h in stages:
            in_flight.append((c, h, scores(c, h)))
            if len(in_flight) > SCORE_LOOKAHEAD:
                accumulate(*in_flight.pop(0))
        for item in in_flight:
            accumulate(*item)
        for h in range(n_heads):
            oT_ref[head_rows(h), :] = acc_ref[h, 0:HEAD_DIM, :] / acc_ref[h, HEAD_DIM:HEAD_DIM + 1, :]

    @pl.when(jnp.logical_not(bounded))
    def _():
        def per_head(h, carry):
            g = h // q_per_kv
            qh = qT_ref[0, pl.ds(pl.multiple_of(h * HEAD_DIM, HEAD_DIM), HEAD_DIM), :]
            zero = jnp.zeros_like(qh)
            qpad = jnp.concatenate([jnp.where(g == gg, qh, zero) for gg in range(N_KV_HEADS)], axis=0)

            def chunk(c, state):
                m, l, acc = state
                ks = k_ref[0, pl.ds(pl.multiple_of(c * tk, tk), tk), :]
                vs = vT_ref[0, c, pl.ds(pl.multiple_of(g * HEAD_DIM, HEAD_DIM), HEAD_DIM), :]
                s = jnp.dot(ks, qpad, preferred_element_type=F32)
                m_new = jnp.maximum(m, jnp.max(s, axis=0, keepdims=True))
                alpha = jnp.exp2(m - m_new)
                p = jnp.exp2(s - m_new)
                l = alpha * l + jnp.sum(p, axis=0, keepdims=True)
                acc = alpha * acc + jnp.dot(vs, p.astype(BF16), preferred_element_type=F32)
                return m_new, l, acc

            init = (jnp.full((1, tq), -jnp.inf, F32), jnp.zeros((1, tq), F32), jnp.zeros((HEAD_DIM, tq), F32))
            _, l, acc = lax.fori_loop(0, n_chunks, chunk, init)
            oT_ref[pl.ds(pl.multiple_of(h * HEAD_DIM, HEAD_DIM), HEAD_DIM), :] = acc / l
            return carry

        lax.fori_loop(0, n_heads, per_head, 0)

    att = oT_ref[...].T.astype(BF16)
    mix = jnp.dot(att, wout_ref[0:att_w, :], preferred_element_type=F32)
    mix = mix + jnp.dot(pool_ref[0], wout_ref[att_w:, :], preferred_element_type=F32)
    g1 = mod_ref[0, 2:3, :]
    x1_ref[0] = x_ref[0] + g1 * mix


def _attention(qT, k, vT, pool, x, mod, w_out, qn, kn, *, tq):
    B, T, D = x.shape
    att_w = qT.shape[1]
    pool_w = pool.shape[2]
    return pl.pallas_call(
        _attn_kernel,
        grid=(B, T // tq),
        in_specs=[pl.BlockSpec((1, att_w, tq), lambda b, i: (b, 0, i)),
                  pl.BlockSpec((1,) + k.shape[1:], lambda b, i: (b, 0, 0)),
                  pl.BlockSpec((1,) + vT.shape[1:], lambda b, i: (b, 0, 0, 0)),
                  pl.BlockSpec((1, tq, pool_w), lambda b, i: (b, i, 0)),
                  pl.BlockSpec((1, tq, D), lambda b, i: (b, i, 0)),
                  pl.BlockSpec((1, N_MOD, D), lambda b, i: (b, 0, 0)),
                  pl.BlockSpec(w_out.shape, lambda b, i: (0, 0)),
                  pl.BlockSpec((1, qn.shape[1], tq), lambda b, i: (b, 0, i)),
                  pl.BlockSpec((1,) + kn.shape[1:], lambda b, i: (b, 0, 0))],
        out_specs=pl.BlockSpec((1, tq, D), lambda b, i: (b, i, 0)),
        out_shape=jax.ShapeDtypeStruct((B, T, D), F32),
        scratch_shapes=[pltpu.VMEM((att_w, tq), F32),
                        pltpu.VMEM((att_w // HEAD_DIM, HEAD_DIM + ONES_ROWS, tq), F32)],
        compiler_params=pltpu.CompilerParams(vmem_limit_bytes=VMEM_LIMIT),
        name="gqa_outproj",
    )(qT, k, vT, pool, x, mod, w_out, qn, kn)


def _ffn_kernel(x_ref, xp_ref, xn_ref, mod_ref, g2_ref, wup_ref, cw_ref, cb_ref, wdn_ref, o_ref,
                h_ref, acc_ref, u_ref, y_ref, *, stage_w):
    i = pl.program_id(1)
    n_i = pl.num_programs(1)
    tf = x_ref.shape[1]
    rows = tf + 2 * HALO
    d_ff = wdn_ref.shape[0]

    sh2 = mod_ref[0, 3:4, :]
    sc2 = mod_ref[0, 4:5, :]
    gate2 = mod_ref[0, 5:6, :]
    g2 = g2_ref[...]
    has_prev = (i > 0).astype(F32)
    has_next = (i < n_i - 1).astype(F32)
    h_ref[...] = jnp.concatenate(
        [_rms_modulate(xp_ref[0], g2, sc2, sh2) * has_prev,
         _rms_modulate(x_ref[0], g2, sc2, sh2),
         _rms_modulate(xn_ref[0], g2, sc2, sh2) * has_next], axis=0).astype(BF16)

    half = tf // 2

    def up(slot, s0, w):
        hv = h_ref[...]
        for part, c0 in enumerate((s0, d_ff + s0)):
            u = jnp.dot(hv, wup_ref[:, c0:c0 + w], preferred_element_type=F32)
            for k in range(w // LANES):
                u_ref[slot, part, k] = u[:, k * LANES:(k + 1) * LANES]

    def conv(slot, part, k, c0, parity):
        cs = slice(c0 + k * LANES, c0 + (k + 1) * LANES)
        taps = [u_ref[slot, part, k, pl.ds(HALO - 1 + parity + j, half, stride=2), :] for j in range(CONV_W)]
        y = taps[0] * cw_ref[0:1, cs] + taps[1] * cw_ref[1:2, cs] + taps[2] * cw_ref[2:3, cs]
        return y + cb_ref[:, cs]

    def down(slot, s0, w):
        blocks = []
        for parity in range(2):
            cols = []
            for k in range(w // LANES):
                gate = conv(slot, 0, k, s0, parity)
                val = conv(slot, 1, k, d_ff + s0, parity)
                cols.append((gate * jax.nn.sigmoid(gate) * val).astype(BF16))
            blocks.append(jnp.concatenate(cols, axis=1))
        a = jnp.concatenate(blocks, axis=0)
        part = jnp.dot(a, wdn_ref[s0:s0 + w, :], preferred_element_type=F32)
        if s0 == 0:
            acc_ref[...] = part
        else:
            acc_ref[...] += part

    stages = [(s0, min(stage_w, d_ff - s0)) for s0 in range(0, d_ff, stage_w)]
    pending = None
    for n, (s0, w) in enumerate(stages):
        up(n % 2, s0, w)
        if pending is not None:
            down(*pending)
        pending = (n % 2, s0, w)
    down(*pending)

    for k in range(acc_ref.shape[1] // LANES):
        cs = slice(k * LANES, (k + 1) * LANES)
        for parity in range(2):
            y_ref[k, pl.ds(parity, half, stride=2), :] = acc_ref[parity * half:(parity + 1) * half, cs]
    for k in range(acc_ref.shape[1] // LANES):
        cs = slice(k * LANES, (k + 1) * LANES)
        o_ref[0, :, cs] = x_ref[0, :, cs] + gate2[:, cs] * y_ref[k]


def _ffn(x1, mod, g2, w_up, conv_w, conv_b, w_down, *, tf, stage_w):
    B, T, D = x1.shape
    n_t = T // tf
    d_ff = w_down.shape[0]
    prev_spec, next_spec = _halo_specs(tf, n_t, D)
    const2 = lambda b, i: (0, 0)
    resident = dict(pipeline_mode=pl.Buffered(1))
    kern = functools.partial(_ffn_kernel, stage_w=stage_w)
    return pl.pallas_call(
        kern,
        grid=(B, n_t),
        in_specs=[pl.BlockSpec((1, tf, D), lambda b, i: (b, i, 0)),
                  prev_spec, next_spec,
                  pl.BlockSpec((1, N_MOD, D), lambda b, i: (b, 0, 0)),
                  pl.BlockSpec((1, D), const2),
                  pl.BlockSpec(w_up.shape, const2, **resident),
                  pl.BlockSpec(conv_w.shape, const2),
                  pl.BlockSpec((1, 2 * d_ff), const2),
                  pl.BlockSpec(w_down.shape, const2, **resident)],
        out_specs=pl.BlockSpec((1, tf, D), lambda b, i: (b, i, 0)),
        out_shape=jax.ShapeDtypeStruct((B, T, D), F32),
        scratch_shapes=[pltpu.VMEM((tf + 2 * HALO, D), BF16),
                        pltpu.VMEM((tf, D), F32),
                        pltpu.VMEM((2, 2, stage_w // LANES, tf + 2 * HALO, LANES), F32),
                        pltpu.VMEM((D // LANES, tf, LANES), F32)],
        compiler_params=pltpu.CompilerParams(vmem_limit_bytes=VMEM_LIMIT),
        name="conv_gated_ffn",
    )(x1, x1, x1, mod, g2, w_up, conv_w, conv_b, w_down)


def _rope_tables(T):
    axis_dim = HEAD_DIM // 2
    pos = jnp.arange(T)
    row = (pos // GRID_W).astype(F32)
    col = (pos % GRID_W).astype(F32)
    d = jnp.arange(HEAD_DIM)
    inv = 1.0 / (ROPE_THETA ** (jnp.arange(0, axis_dim, 2, dtype=F32) / axis_dim))
    freq = inv[d % (axis_dim // 2)]
    p = jnp.where((d // axis_dim == 0)[:, None], row[None, :], col[None, :])
    ang = p * freq[:, None]
    sign = jnp.where((d % axis_dim) < axis_dim // 2, -1.0, 1.0).astype(F32)
    return jnp.cos(ang), jnp.sin(ang) * sign[:, None]


def _head_gain_rows(gain):
    return jnp.broadcast_to(jnp.tile(gain, LANES // HEAD_DIM)[:, None], (LANES, LANES))


def kernel(x, c, w_ada, b_ada, norm1_g, w_in, q_norm_g, k_norm_g, w_pool, pool_scale, w_out, norm2_g,
           w_up, conv_w, conv_b, w_down):
    B, T, D = x.shape
    depth = w_ada.shape[0]
    att_w = D // 2
    tile = 512
    assert T % tile == 0 and T % GRID_W == 0 and D % LANES == 0
    d_ff = w_down.shape[1]
    q_scale = math.log2(math.e) / math.sqrt(HEAD_DIM)
    cos_t, sin_t = _rope_tables(T)

    for l in range(depth):
        mod = _modulation(c, w_ada[l], b_ada[l]).reshape(B, N_MOD, D)
        qT, k, vT, pool, qn, kn = _projection(
            x, mod, norm1_g[l].reshape(1, D), w_in[l].astype(BF16),
            _head_gain_rows(q_norm_g[l] * q_scale), _head_gain_rows(k_norm_g[l]),
            cos_t, sin_t, w_pool[l].astype(BF16), pool_scale[l].reshape(1, -1),
            tt=tile, att_w=att_w)
        x1 = _attention(qT, k, vT, pool, x, mod, w_out[l].astype(BF16), qn, kn, tq=tile)
        x = _ffn(x1, mod, norm2_g[l].reshape(1, D), w_up[l].astype(BF16), conv_w[l],
                 conv_b[l].reshape(1, -1), w_down[l].astype(BF16), tf=tile, stage_w=512)
    return x
```

```python
import functools
import math

import jax
import jax.numpy as jnp
from jax import lax
from jax.experimental import pallas as pl
from jax.experimental.pallas import tpu as pltpu

HEAD_DIM = 64
N_KV_HEADS = 2
GRID_W = 64
ROPE_THETA = 10000.0
POOL_WINDOWS = (2, 4, 8, 16)
CONV_W = 3
N_MOD = 6
EPS = 1e-6

LANES = 128
HALO = 8
VMEM_LIMIT = 56 * 1024 * 1024
SCORE_LOOKAHEAD = 2
KEY_STAGE = 256
SCORE_BOUND = 64.0
ONES_ROWS = 16

F32 = jnp.float32
BF16 = jnp.bfloat16


def _rms_modulate(xv, g, sc, sh):
    ms = jnp.mean(xv * xv, axis=-1, keepdims=True)
    return (xv * lax.rsqrt(ms + EPS) * g) * (1.0 + sc) + sh


def _halo_specs(tile, n_tiles, width):
    per = tile // HALO
    last = n_tiles * per - 1
    prev = pl.BlockSpec((1, HALO, width), lambda b, i: (b, jnp.maximum(i * per - 1, 0), 0))
    nxt = pl.BlockSpec((1, HALO, width), lambda b, i: (b, jnp.minimum((i + 1) * per, last), 0))
    return prev, nxt


def _mod_kernel(c_ref, w_ref, b_ref, o_ref):
    c = c_ref[...]
    c_act = c * jax.nn.sigmoid(c)
    o_ref[...] = jnp.dot(c_act, w_ref[...], preferred_element_type=F32) + b_ref[...]


def _modulation(c, w_ada, b_ada):
    B, D = c.shape
    n = w_ada.shape[1]
    bn = D
    return pl.pallas_call(
        _mod_kernel,
        grid=(n // bn,),
        in_specs=[pl.BlockSpec((B, D), lambda j: (0, 0)),
                  pl.BlockSpec((D, bn), lambda j: (0, j)),
                  pl.BlockSpec((1, bn), lambda j: (0, j))],
        out_specs=pl.BlockSpec((B, bn), lambda j: (0, j)),
        out_shape=jax.ShapeDtypeStruct((B, n), F32),
        compiler_params=pltpu.CompilerParams(vmem_limit_bytes=VMEM_LIMIT),
        name="adaln_modulation",
    )(c, w_ada, b_ada.reshape(1, n))


def _proj_kernel(x_ref, xp_ref, xn_ref, mod_ref, g1_ref, win_ref, gq_ref, gk_ref, cos_ref, sin_ref,
                 wpool_ref, pscale_ref, qT_ref, k_ref, vT_ref, pool_ref, qn_ref, kn_ref, u_ref, y_ref, *, seq_len):
    i = pl.program_id(1)
    n_i = pl.num_programs(1)
    tt = x_ref.shape[1]
    half = tt // 2
    att_w = qT_ref.shape[1]
    kv_w = N_KV_HEADS * HEAD_DIM
    pool_w = pool_ref.shape[2]
    pool0 = att_w + 2 * kv_w

    sh1 = mod_ref[0, 0:1, :]
    sc1 = mod_ref[0, 1:2, :]
    g1 = g1_ref[...]
    h = _rms_modulate(x_ref[0], g1, sc1, sh1).astype(BF16)
    has_prev = (i > 0).astype(F32)
    has_next = (i < n_i - 1).astype(F32)
    h_halo = jnp.concatenate([_rms_modulate(xp_ref[0], g1, sc1, sh1) * has_prev,
                              _rms_modulate(xn_ref[0], g1, sc1, sh1) * has_next], axis=0).astype(BF16)

    w_pool_in = win_ref[:, pool0:pool0 + pool_w]
    u_main = jnp.dot(h, w_pool_in, preferred_element_type=F32)
    u_halo = jnp.dot(h_halo, w_pool_in, preferred_element_type=F32)
    qkv = jnp.dot(h, win_ref[:, 0:pool0], preferred_element_type=F32)
    gdim = pool_w // len(POOL_WINDOWS)
    for g in range(len(POOL_WINDOWS)):
        cs = slice(g * gdim, (g + 1) * gdim)
        u_ref[g, 0:HALO, :] = u_halo[0:HALO, cs]
        u_ref[g, HALO:HALO + tt, :] = u_main[:, cs]
        u_ref[g, HALO + tt:, :] = u_halo[HALO:, cs]

    j2 = 2 * lax.broadcasted_iota(jnp.int32, (half, 1), 0)
    for g, w in enumerate(POOL_WINDOWS):
        cs = slice(g * gdim, (g + 1) * gdim)
        halves = []
        for parity in range(2):
            taps = [u_ref[g, pl.ds(HALO + o + parity, half, stride=2), :] for o in range(-w // 2, w // 2)]
            win = taps[0]
            for tap in taps[1:]:
                win = win + tap
            t = i * tt + j2 + parity
            lo = jnp.maximum(t - w // 2, 0)
            hi = jnp.minimum(t + w // 2 - 1, seq_len - 1)
            cnt = (hi - lo + 1).astype(F32)
            halves.append(win / cnt - taps[w // 2])
        pooled = jnp.concatenate(halves, axis=0).astype(BF16)
        mixed = jnp.dot(pooled, wpool_ref[g], preferred_element_type=F32) * pscale_ref[:, cs]
        for parity in range(2):
            y_ref[g, pl.ds(parity, half, stride=2), :] = mixed[parity * half:(parity + 1) * half]
        pool_ref[0, :, cs] = y_ref[g].astype(BF16)

    cos = cos_ref[...]
    sin = sin_ref[...]
    n_col = tt // LANES

    def norm_rope_t(blk, gain_ref):
        xt = blk.T
        gain = jnp.concatenate([gain_ref[...]] * n_col, axis=1)
        outs = []
        for hh in range(LANES // HEAD_DIM):
            rows = slice(hh * HEAD_DIM, (hh + 1) * HEAD_DIM)
            xh = xt[rows, :]
            ms = jnp.mean(xh * xh, axis=0, keepdims=True)
            y = xh * lax.rsqrt(ms + EPS) * gain[rows, :]
            q4 = HEAD_DIM // 4
            partner = jnp.concatenate([y[q4:2 * q4], y[0:q4], y[3 * q4:], y[2 * q4:3 * q4]], axis=0)
            outs.append(y * cos + partner * sin)
        return outs

    def sq_norm(rows_t):
        return jnp.sum(rows_t * rows_t, axis=0, keepdims=True)

    for blk in range(att_w // LANES):
        for hh, qt in enumerate(norm_rope_t(qkv[:, blk * LANES:(blk + 1) * LANES], gq_ref)):
            h = blk * (LANES // HEAD_DIM) + hh
            qT_ref[0, h * HEAD_DIM:(h + 1) * HEAD_DIM, :] = qt.astype(BF16)
            qn_ref[0, h:h + 1, :] = sq_norm(qt)

    k_heads = norm_rope_t(qkv[:, att_w:att_w + kv_w], gk_ref)
    for g, kt in enumerate(k_heads):
        kn_ref[0, g:g + 1, :] = sq_norm(kt)
    k_ref[0] = jnp.concatenate(k_heads, axis=0).T.astype(BF16)

    vT_ref[0, 0] = qkv[:, att_w + kv_w:pool0].T.astype(BF16)


def _projection(x, mod, g1, w_in, gq, gk, cos_t, sin_t, w_pool, pool_scale, *, tt, att_w):
    B, T, D = x.shape
    n_t = T // tt
    in_w = w_in.shape[1]
    kv_w = N_KV_HEADS * HEAD_DIM
    pool_w = in_w - att_w - 2 * kv_w
    prev_spec, next_spec = _halo_specs(tt, n_t, D)
    const2 = lambda b, i: (0, 0)
    kern = functools.partial(_proj_kernel, seq_len=T)
    return pl.pallas_call(
        kern,
        grid=(B, n_t),
        in_specs=[pl.BlockSpec((1, tt, D), lambda b, i: (b, i, 0)),
                  prev_spec, next_spec,
                  pl.BlockSpec((1, N_MOD, D), lambda b, i: (b, 0, 0)),
                  pl.BlockSpec((1, D), const2),
                  pl.BlockSpec((D, in_w), const2),
                  pl.BlockSpec((LANES, LANES), const2),
                  pl.BlockSpec((LANES, LANES), const2),
                  pl.BlockSpec((HEAD_DIM, tt), lambda b, i: (0, i)),
                  pl.BlockSpec((HEAD_DIM, tt), lambda b, i: (0, i)),
                  pl.BlockSpec(w_pool.shape, lambda b, i: (0, 0, 0)),
                  pl.BlockSpec((1, pool_w), const2)],
        out_specs=[pl.BlockSpec((1, att_w, tt), lambda b, i: (b, 0, i)),
                   pl.BlockSpec((1, tt, kv_w), lambda b, i: (b, i, 0)),
                   pl.BlockSpec((1, 1, kv_w, tt), lambda b, i: (b, i, 0, 0)),
                   pl.BlockSpec((1, tt, pool_w), lambda b, i: (b, i, 0)),
                   pl.BlockSpec((1, att_w // HEAD_DIM, tt), lambda b, i: (b, 0, i)),
                   pl.BlockSpec((1, N_KV_HEADS, tt), lambda b, i: (b, 0, i))],
        out_shape=[jax.ShapeDtypeStruct((B, att_w, T), BF16),
                   jax.ShapeDtypeStruct((B, T, kv_w), BF16),
                   jax.ShapeDtypeStruct((B, n_t, kv_w, tt), BF16),
                   jax.ShapeDtypeStruct((B, T, pool_w), BF16),
                   jax.ShapeDtypeStruct((B, att_w // HEAD_DIM, T), F32),
                   jax.ShapeDtypeStruct((B, N_KV_HEADS, T), F32)],
        scratch_shapes=[pltpu.VMEM((len(POOL_WINDOWS), tt + 2 * HALO, LANES), F32),
                        pltpu.VMEM((len(POOL_WINDOWS), tt, LANES), F32)],
        compiler_params=pltpu.CompilerParams(vmem_limit_bytes=VMEM_LIMIT),
        name="proj_qkv_pool",
    )(x, x, x, mod, g1, w_in, gq, gk, cos_t, sin_t, w_pool, pool_scale)


def _attn_kernel(qT_ref, k_ref, vT_ref, pool_ref, x_ref, mod_ref, wout_ref, qn_ref, kn_ref, x1_ref,
                 oT_ref, acc_ref):
    n_chunks = vT_ref.shape[1]
    tk = vT_ref.shape[3]
    tq = qT_ref.shape[2]
    att_w = qT_ref.shape[1]
    n_heads = att_w // HEAD_DIM
    q_per_kv = n_heads // N_KV_HEADS

    def head_rows(h):
        return slice(h * HEAD_DIM, (h + 1) * HEAD_DIM)

    def padded_q(qh, g):
        zero = jnp.zeros_like(qh)
        return jnp.concatenate([qh if gg == g else zero for gg in range(N_KV_HEADS)], axis=0)

    bound_sq = jnp.max(qn_ref[0]) * jnp.max(kn_ref[0])
    bounded = bound_sq <= SCORE_BOUND * SCORE_BOUND

    @pl.when(bounded)
    def _():
        ones = jnp.ones((ONES_ROWS, KEY_STAGE), BF16)
        per_chunk = tk // KEY_STAGE

        def scores(c, h):
            ks = k_ref[0, c * KEY_STAGE:(c + 1) * KEY_STAGE, :]
            return jnp.dot(ks, padded_q(qT_ref[0, head_rows(h), :], h // q_per_kv),
                           preferred_element_type=F32)

        def accumulate(c, h, s):
            g = h // q_per_kv
            lanes = slice((c % per_chunk) * KEY_STAGE, (c % per_chunk + 1) * KEY_STAGE)
            v_aug = jnp.concatenate([vT_ref[0, c // per_chunk, head_rows(g), lanes], ones], axis=0)
            pv = jnp.dot(v_aug, jnp.exp2(s).astype(BF16), preferred_element_type=F32)
            if c == 0:
                acc_ref[h] = pv
            else:
                acc_ref[h] += pv

        stages = [(c, g * q_per_kv + hh) for g in range(N_KV_HEADS) for c in range(n_chunks * per_chunk)
                  for hh in range(q_per_kv)]
        in_flight = []
        for c, h in stages:
            in_flight.append((c, h, scores(c, h)))
            if len(in_flight) > SCORE_LOOKAHEAD:
                accumulate(*in_flight.pop(0))
        for item in in_flight:
            accumulate(*item)
        for h in range(n_heads):
            oT_ref[head_rows(h), :] = acc_ref[h, 0:HEAD_DIM, :] / acc_ref[h, HEAD_DIM:HEAD_DIM + 1, :]

    @pl.when(jnp.logical_not(bounded))
    def _():
        def per_head(h, carry):
            g = h // q_per_kv
            qh = qT_ref[0, pl.ds(pl.multiple_of(h * HEAD_DIM, HEAD_DIM), HEAD_DIM), :]
            zero = jnp.zeros_like(qh)
            qpad = jnp.concatenate([jnp.where(g == gg, qh, zero) for gg in range(N_KV_HEADS)], axis=0)

            def chunk(c, state):
                m, l, acc = state
                ks = k_ref[0, pl.ds(pl.multiple_of(c * tk, tk), tk), :]
                vs = vT_ref[0, c, pl.ds(pl.multiple_of(g * HEAD_DIM, HEAD_DIM), HEAD_DIM), :]
                s = jnp.dot(ks, qpad, preferred_element_type=F32)
                m_new = jnp.maximum(m, jnp.max(s, axis=0, keepdims=True))
                alpha = jnp.exp2(m - m_new)
                p = jnp.exp2(s - m_new)
                l = alpha * l + jnp.sum(p, axis=0, keepdims=True)
                acc = alpha * acc + jnp.dot(vs, p.astype(BF16), preferred_element_type=F32)
                return m_new, l, acc

            init = (jnp.full((1, tq), -jnp.inf, F32), jnp.zeros((1, tq), F32), jnp.zeros((HEAD_DIM, tq), F32))
            _, l, acc = lax.fori_loop(0, n_chunks, chunk, init)
            oT_ref[pl.ds(pl.multiple_of(h * HEAD_DIM, HEAD_DIM), HEAD_DIM), :] = acc / l
            return carry

        lax.fori_loop(0, n_heads, per_head, 0)

    att = oT_ref[...].T.astype(BF16)
    mix = jnp.dot(att, wout_ref[0:att_w, :], preferred_element_type=F32)
    mix = mix + jnp.dot(pool_ref[0], wout_ref[att_w:, :], preferred_element_type=F32)
    g1 = mod_ref[0, 2:3, :]
    x1_ref[0] = x_ref[0] + g1 * mix


def _attention(qT, k, vT, pool, x, mod, w_out, qn, kn, *, tq):
    B, T, D = x.shape
    att_w = qT.shape[1]
    pool_w = pool.shape[2]
    return pl.pallas_call(
        _attn_kernel,
        grid=(B, T // tq),
        in_specs=[pl.BlockSpec((1, att_w, tq), lambda b, i: (b, 0, i)),
                  pl.BlockSpec((1,) + k.shape[1:], lambda b, i: (b, 0, 0)),
                  pl.BlockSpec((1,) + vT.shape[1:], lambda b, i: (b, 0, 0, 0)),
                  pl.BlockSpec((1, tq, pool_w), lambda b, i: (b, i, 0)),
                  pl.BlockSpec((1, tq, D), lambda b, i: (b, i, 0)),
                  pl.BlockSpec((1, N_MOD, D), lambda b, i: (b, 0, 0)),
                  pl.BlockSpec(w_out.shape, lambda b, i: (0, 0)),
                  pl.BlockSpec((1, qn.shape[1], tq), lambda b, i: (b, 0, i)),
                  pl.BlockSpec((1,) + kn.shape[1:], lambda b, i: (b, 0, 0))],
        out_specs=pl.BlockSpec((1, tq, D), lambda b, i: (b, i, 0)),
        out_shape=jax.ShapeDtypeStruct((B, T, D), F32),
        scratch_shapes=[pltpu.VMEM((att_w, tq), F32),
                        pltpu.VMEM((att_w // HEAD_DIM, HEAD_DIM + ONES_ROWS, tq), F32)],
        compiler_params=pltpu.CompilerParams(vmem_limit_bytes=VMEM_LIMIT),
        name="gqa_outproj",
    )(qT, k, vT, pool, x, mod, w_out, qn, kn)


def _ffn_kernel(x_ref, xp_ref, xn_ref, mod_ref, g2_ref, wup_ref, cw_ref, cb_ref, wdn_ref, o_ref,
                h_ref, acc_ref, u_ref, y_ref, *, stage_w):
    i = pl.program_id(1)
    n_i = pl.num_programs(1)
    tf = x_ref.shape[1]
    rows = tf + 2 * HALO
    d_ff = wdn_ref.shape[0]

    sh2 = mod_ref[0, 3:4, :]
    sc2 = mod_ref[0, 4:5, :]
    gate2 = mod_ref[0, 5:6, :]
    g2 = g2_ref[...]
    has_prev = (i > 0).astype(F32)
    has_next = (i < n_i - 1).astype(F32)
    h_ref[...] = jnp.concatenate(
        [_rms_modulate(xp_ref[0], g2, sc2, sh2) * has_prev,
         _rms_modulate(x_ref[0], g2, sc2, sh2),
         _rms_modulate(xn_ref[0], g2, sc2, sh2) * has_next], axis=0).astype(BF16)

    half = tf // 2

    def up(slot, s0, w):
        hv = h_ref[...]
        for part, c0 in enumerate((s0, d_ff + s0)):
            u = jnp.dot(hv, wup_ref[:, c0:c0 + w], preferred_element_type=F32)
            for k in range(w // LANES):
                u_ref[slot, part, k] = u[:, k * LANES:(k + 1) * LANES]

    def conv(slot, part, k, c0, parity):
        cs = slice(c0 + k * LANES, c0 + (k + 1) * LANES)
        taps = [u_ref[slot, part, k, pl.ds(HALO - 1 + parity + j, half, stride=2), :] for j in range(CONV_W)]
        y = taps[0] * cw_ref[0:1, cs] + taps[1] * cw_ref[1:2, cs] + taps[2] * cw_ref[2:3, cs]
        return y + cb_ref[:, cs]

    def down(slot, s0, w):
        blocks = []
        for parity in range(2):
            cols = []
            for k in range(w // LANES):
                gate = conv(slot, 0, k, s0, parity)
                val = conv(slot, 1, k, d_ff + s0, parity)
                cols.append((gate * jax.nn.sigmoid(gate) * val).astype(BF16))
            blocks.append(jnp.concatenate(cols, axis=1))
        a = jnp.concatenate(blocks, axis=0)
        part = jnp.dot(a, wdn_ref[s0:s0 + w, :], preferred_element_type=F32)
        if s0 == 0:
            acc_ref[...] = part
        else:
            acc_ref[...] += part

    stages = [(s0, min(stage_w, d_ff - s0)) for s0 in range(0, d_ff, stage_w)]
    pending = None
    for n, (s0, w) in enumerate(stages):
        up(n % 2, s0, w)
        if pending is not None:
            down(*pending)
        pending = (n % 2, s0, w)
    down(*pending)

    for k in range(acc_ref.shape[1] // LANES):
        cs = slice(k * LANES, (k + 1) * LANES)
        for parity in range(2):
            y_ref[k, pl.ds(parity, half, stride=2), :] = acc_ref[parity * half:(parity + 1) * half, cs]
    for k in range(acc_ref.shape[1] // LANES):
        cs = slice(k * LANES, (k + 1) * LANES)
        o_ref[0, :, cs] = x_ref[0, :, cs] + gate2[:, cs] * y_ref[k]


def _ffn(x1, mod, g2, w_up, conv_w, conv_b, w_down, *, tf, stage_w):
    B, T, D = x1.shape
    n_t = T // tf
    d_ff = w_down.shape[0]
    prev_spec, next_spec = _halo_specs(tf, n_t, D)
    const2 = lambda b, i: (0, 0)
    resident = dict(pipeline_mode=pl.Buffered(1))
    kern = functools.partial(_ffn_kernel, stage_w=stage_w)
    return pl.pallas_call(
        kern,
        grid=(B, n_t),
        in_specs=[pl.BlockSpec((1, tf, D), lambda b, i: (b, i, 0)),
                  prev_spec, next_spec,
                  pl.BlockSpec((1, N_MOD, D), lambda b, i: (b, 0, 0)),
                  pl.BlockSpec((1, D), const2),
                  pl.BlockSpec(w_up.shape, const2, **resident),
                  pl.BlockSpec(conv_w.shape, const2),
                  pl.BlockSpec((1, 2 * d_ff), const2),
                  pl.BlockSpec(w_down.shape, const2, **resident)],
        out_specs=pl.BlockSpec((1, tf, D), lambda b, i: (b, i, 0)),
        out_shape=jax.ShapeDtypeStruct((B, T, D), F32),
        scratch_shapes=[pltpu.VMEM((tf + 2 * HALO, D), BF16),
                        pltpu.VMEM((tf, D), F32),
                        pltpu.VMEM((2, 2, stage_w // LANES, tf + 2 * HALO, LANES), F32),
                        pltpu.VMEM((D // LANES, tf, LANES), F32)],
        compiler_params=pltpu.CompilerParams(vmem_limit_bytes=VMEM_LIMIT),
        name="conv_gated_ffn",
    )(x1, x1, x1, mod, g2, w_up, conv_w, conv_b, w_down)


def _rope_tables(T):
    axis_dim = HEAD_DIM // 2
    pos = jnp.arange(T)
    row = (pos // GRID_W).astype(F32)
    col = (pos % GRID_W).astype(F32)
    d = jnp.arange(HEAD_DIM)
    inv = 1.0 / (ROPE_THETA ** (jnp.arange(0, axis_dim, 2, dtype=F32) / axis_dim))
    freq = inv[d % (axis_dim // 2)]
    p = jnp.where((d // axis_dim == 0)[:, None], row[None, :], col[None, :])
    ang = p * freq[:, None]
    sign = jnp.where((d % axis_dim) < axis_dim // 2, -1.0, 1.0).astype(F32)
    return jnp.cos(ang), jnp.sin(ang) * sign[:, None]


def _head_gain_rows(gain):
    return jnp.broadcast_to(jnp.tile(gain, LANES // HEAD_DIM)[:, None], (LANES, LANES))


def kernel(x, c, w_ada, b_ada, norm1_g, w_in, q_norm_g, k_norm_g, w_pool, pool_scale, w_out, norm2_g,
           w_up, conv_w, conv_b, w_down):
    B, T, D = x.shape
    depth = w_ada.shape[0]
    att_w = D // 2
    tile = 512
    assert T % tile == 0 and T % GRID_W == 0 and D % LANES == 0
    d_ff = w_down.shape[1]
    q_scale = math.log2(math.e) / math.sqrt(HEAD_DIM)
    cos_t, sin_t = _rope_tables(T)

    for l in range(depth):
        mod = _modulation(c, w_ada[l], b_ada[l]).reshape(B, N_MOD, D)
        qT, k, vT, pool, qn, kn = _projection(
            x, mod, norm1_g[l].reshape(1, D), w_in[l].astype(BF16),
            _head_gain_rows(q_norm_g[l] * q_scale), _head_gain_rows(k_norm_g[l]),
            cos_t, sin_t, w_pool[l].astype(BF16), pool_scale[l].reshape(1, -1),
            tt=tile, att_w=att_w)
        x1 = _attention(qT, k, vT, pool, x, mod, w_out[l].astype(BF16), qn, kn, tq=tile)
        x = _ffn(x1, mod, norm2_g[l].reshape(1, D), w_up[l].astype(BF16), conv_w[l],
                 conv_b[l].reshape(1, -1), w_down[l].astype(BF16), tf=tile, stage_w=512)
    return x
```

```python
import functools
import math

import jax
import jax.numpy as jnp
from jax import lax
from jax.experimental import pallas as pl
from jax.experimental.pallas import tpu as pltpu

HEAD_DIM = 64
N_KV_HEADS = 2
GRID_W = 64
ROPE_THETA = 10000.0
POOL_WINDOWS = (2, 4, 8, 16)
CONV_W = 3
N_MOD = 6
EPS = 1e-6

LANES = 128
HALO = 8
VMEM_LIMIT = 56 * 1024 * 1024
SCORE_LOOKAHEAD = 2
KEY_STAGE = 256
SCORE_BOUND = 64.0
ONES_ROWS = 16

F32 = jnp.float32
BF16 = jnp.bfloat16


def _rms_modulate(xv, g, sc, sh):
    ms = jnp.mean(xv * xv, axis=-1, keepdims=True)
    return xv * lax.rsqrt(ms + EPS) * (g * (1.0 + sc)) + sh


def _halo_specs(tile, n_tiles, width):
    per = tile // HALO
    last = n_tiles * per - 1
    prev = pl.BlockSpec((1, HALO, width), lambda b, i: (b, jnp.maximum(i * per - 1, 0), 0))
    nxt = pl.BlockSpec((1, HALO, width), lambda b, i: (b, jnp.minimum((i + 1) * per, last), 0))
    return prev, nxt


def _mod_kernel(c_ref, w_ref, b_ref, o_ref):
    c = c_ref[...]
    c_act = c * jax.nn.sigmoid(c)
    o_ref[...] = jnp.dot(c_act, w_ref[...], preferred_element_type=F32) + b_ref[...]


def _modulation(c, w_ada, b_ada):
    B, D = c.shape
    n = w_ada.shape[1]
    bn = D
    return pl.pallas_call(
        _mod_kernel,
        grid=(n // bn,),
        in_specs=[pl.BlockSpec((B, D), lambda j: (0, 0)),
                  pl.BlockSpec((D, bn), lambda j: (0, j)),
                  pl.BlockSpec((1, bn), lambda j: (0, j))],
        out_specs=pl.BlockSpec((B, bn), lambda j: (0, j)),
        out_shape=jax.ShapeDtypeStruct((B, n), F32),
        compiler_params=pltpu.CompilerParams(vmem_limit_bytes=VMEM_LIMIT),
        name="adaln_modulation",
    )(c, w_ada, b_ada.reshape(1, n))


def _proj_kernel(x_ref, xp_ref, xn_ref, mod_ref, g1_ref, win_ref, gq_ref, gk_ref, cos_ref, sin_ref,
                 wpool_ref, pscale_ref, qT_ref, k_ref, vT_ref, pool_ref, u_ref, y_ref, *, seq_len):
    i = pl.program_id(1)
    n_i = pl.num_programs(1)
    tt = x_ref.shape[1]
    half = tt // 2
    att_w = qT_ref.shape[1]
    kv_w = N_KV_HEADS * HEAD_DIM
    pool_w = pool_ref.shape[2]
    pool0 = att_w + 2 * kv_w

    sh1 = mod_ref[0, 0:1, :]
    sc1 = mod_ref[0, 1:2, :]
    g1 = g1_ref[...]
    h = _rms_modulate(x_ref[0], g1, sc1, sh1).astype(BF16)
    has_prev = (i > 0).astype(F32)
    has_next = (i < n_i - 1).astype(F32)
    h_halo = jnp.concatenate([_rms_modulate(xp_ref[0], g1, sc1, sh1) * has_prev,
                              _rms_modulate(xn_ref[0], g1, sc1, sh1) * has_next], axis=0).astype(BF16)

    w_pool_in = win_ref[:, pool0:pool0 + pool_w]
    u_main = jnp.dot(h, w_pool_in, preferred_element_type=F32)
    u_halo = jnp.dot(h_halo, w_pool_in, preferred_element_type=F32)
    qkv = jnp.dot(h, win_ref[:, 0:pool0], preferred_element_type=F32)
    gdim = pool_w // len(POOL_WINDOWS)
    for g in range(len(POOL_WINDOWS)):
        cs = slice(g * gdim, (g + 1) * gdim)
        u_ref[g, 0:HALO, :] = u_halo[0:HALO, cs]
        u_ref[g, HALO:HALO + tt, :] = u_main[:, cs]
        u_ref[g, HALO + tt:, :] = u_halo[HALO:, cs]

    j2 = 2 * lax.broadcasted_iota(jnp.int32, (half, 1), 0)
    for g, w in enumerate(POOL_WINDOWS):
        cs = slice(g * gdim, (g + 1) * gdim)
        halves = []
        for parity in range(2):
            taps = [u_ref[g, pl.ds(HALO + o + parity, half, stride=2), :] for o in range(-w // 2, w // 2)]
            win = taps[0]
            for tap in taps[1:]:
                win = win + tap
            t = i * tt + j2 + parity
            lo = jnp.maximum(t - w // 2, 0)
            hi = jnp.minimum(t + w // 2 - 1, seq_len - 1)
            cnt = (hi - lo + 1).astype(F32)
            halves.append(win / cnt - taps[w // 2])
        pooled = jnp.concatenate(halves, axis=0).astype(BF16)
        mixed = jnp.dot(pooled, wpool_ref[g], preferred_element_type=F32) * pscale_ref[:, cs]
        for parity in range(2):
            y_ref[g, pl.ds(parity, half, stride=2), :] = mixed[parity * half:(parity + 1) * half]
        pool_ref[0, :, cs] = y_ref[g].astype(BF16)

    cos = cos_ref[...]
    sin = sin_ref[...]
    n_col = tt // LANES

    def norm_rope_t(blk, gain_ref):
        xt = blk.T
        gain = jnp.concatenate([gain_ref[...]] * n_col, axis=1)
        outs = []
        for hh in range(LANES // HEAD_DIM):
            rows = slice(hh * HEAD_DIM, (hh + 1) * HEAD_DIM)
            xh = xt[rows, :]
            ms = jnp.mean(xh * xh, axis=0, keepdims=True)
            y = xh * lax.rsqrt(ms + EPS) * gain[rows, :]
            q4 = HEAD_DIM // 4
            partner = jnp.concatenate([y[q4:2 * q4], y[0:q4], y[3 * q4:], y[2 * q4:3 * q4]], axis=0)
            outs.append(y * cos + partner * sin)
        return outs

    for blk in range(att_w // LANES):
        for hh, qt in enumerate(norm_rope_t(qkv[:, blk * LANES:(blk + 1) * LANES], gq_ref)):
            h = blk * (LANES // HEAD_DIM) + hh
            qT_ref[0, h * HEAD_DIM:(h + 1) * HEAD_DIM, :] = qt.astype(BF16)

    k_heads = norm_rope_t(qkv[:, att_w:att_w + kv_w], gk_ref)
    k_ref[0] = jnp.concatenate(k_heads, axis=0).T.astype(BF16)

    vT_ref[0, 0] = qkv[:, att_w + kv_w:pool0].T.astype(BF16)


def _projection(x, mod, g1, w_in, gq, gk, cos_t, sin_t, w_pool, pool_scale, *, tt, att_w):
    B, T, D = x.shape
    n_t = T // tt
    in_w = w_in.shape[1]
    kv_w = N_KV_HEADS * HEAD_DIM
    pool_w = in_w - att_w - 2 * kv_w
    prev_spec, next_spec = _halo_specs(tt, n_t, D)
    const2 = lambda b, i: (0, 0)
    kern = functools.partial(_proj_kernel, seq_len=T)
    return pl.pallas_call(
        kern,
        grid=(B, n_t),
        in_specs=[pl.BlockSpec((1, tt, D), lambda b, i: (b, i, 0)),
                  prev_spec, next_spec,
                  pl.BlockSpec((1, N_MOD, D), lambda b, i: (b, 0, 0)),
                  pl.BlockSpec((1, D), const2),
                  pl.BlockSpec((D, in_w), const2),
                  pl.BlockSpec((LANES, LANES), const2),
                  pl.BlockSpec((LANES, LANES), const2),
                  pl.BlockSpec((HEAD_DIM, tt), lambda b, i: (0, i)),
                  pl.BlockSpec((HEAD_DIM, tt), lambda b, i: (0, i)),
                  pl.BlockSpec(w_pool.shape, lambda b, i: (0, 0, 0)),
                  pl.BlockSpec((1, pool_w), const2)],
        out_specs=[pl.BlockSpec((1, att_w, tt), lambda b, i: (b, 0, i)),
                   pl.BlockSpec((1, tt, kv_w), lambda b, i: (b, i, 0)),
                   pl.BlockSpec((1, 1, kv_w, tt), lambda b, i: (b, i, 0, 0)),
                   pl.BlockSpec((1, tt, pool_w), lambda b, i: (b, i, 0))],
        out_shape=[jax.ShapeDtypeStruct((B, att_w, T), BF16),
                   jax.ShapeDtypeStruct((B, T, kv_w), BF16),
                   jax.ShapeDtypeStruct((B, n_t, kv_w, tt), BF16),
                   jax.ShapeDtypeStruct((B, T, pool_w), BF16)],
        scratch_shapes=[pltpu.VMEM((len(POOL_WINDOWS), tt + 2 * HALO, LANES), F32),
                        pltpu.VMEM((len(POOL_WINDOWS), tt, LANES), F32)],
        compiler_params=pltpu.CompilerParams(vmem_limit_bytes=VMEM_LIMIT),
        name="proj_qkv_pool",
    )(x, x, x, mod, g1, w_in, gq, gk, cos_t, sin_t, w_pool, pool_scale)


def _attn_kernel(bounded_ref, qT_ref, k_ref, vT_ref, pool_ref, x_ref, mod_ref, wout_ref, x1_ref,
                 oT_ref, acc_ref):
    n_chunks = vT_ref.shape[1]
    tk = vT_ref.shape[3]
    tq = qT_ref.shape[2]
    att_w = qT_ref.shape[1]
    n_heads = att_w // HEAD_DIM
    q_per_kv = n_heads // N_KV_HEADS

    def head_rows(h):
        return slice(h * HEAD_DIM, (h + 1) * HEAD_DIM)

    def padded_q(qh, g):
        zero = jnp.zeros_like(qh)
        return jnp.concatenate([qh if gg == g else zero for gg in range(N_KV_HEADS)], axis=0)

    bounded = bounded_ref[0] == 1

    @pl.when(bounded)
    def _():
        ones = jnp.ones((ONES_ROWS, KEY_STAGE), BF16)
        per_chunk = tk // KEY_STAGE

        def scores(c, h):
            ks = k_ref[0, c * KEY_STAGE:(c + 1) * KEY_STAGE, :]
            return jnp.dot(ks, padded_q(qT_ref[0, head_rows(h), :], h // q_per_kv),
                           preferred_element_type=F32)

        def accumulate(c, h, s):
            g = h // q_per_kv
            lanes = slice((c % per_chunk) * KEY_STAGE, (c % per_chunk + 1) * KEY_STAGE)
            v_aug = jnp.concatenate([vT_ref[0, c // per_chunk, head_rows(g), lanes], ones], axis=0)
            pv = jnp.dot(v_aug, jnp.exp2(s).astype(BF16), preferred_element_type=F32)
            if c == 0:
                acc_ref[h] = pv
            else:
                acc_ref[h] += pv

        stages = [(c, g * q_per_kv + hh) for g in range(N_KV_HEADS) for c in range(n_chunks * per_chunk)
                  for hh in range(q_per_kv)]
        in_flight = []
        for c, h in stages:
            in_flight.append((c, h, scores(c, h)))
            if len(in_flight) > SCORE_LOOKAHEAD:
                accumulate(*in_flight.pop(0))
        for item in in_flight:
            accumulate(*item)
        for h in range(n_heads):
            oT_ref[head_rows(h), :] = acc_ref[h, 0:HEAD_DIM, :] / acc_ref[h, HEAD_DIM:HEAD_DIM + 1, :]

    @pl.when(jnp.logical_not(bounded))
    def _():
        def per_head(h, carry):
            g = h // q_per_kv
            qh = qT_ref[0, pl.ds(pl.multiple_of(h * HEAD_DIM, HEAD_DIM), HEAD_DIM), :]
            zero = jnp.zeros_like(qh)
            qpad = jnp.concatenate([jnp.where(g == gg, qh, zero) for gg in range(N_KV_HEADS)], axis=0)

            def chunk(c, state):
                m, l, acc = state
                ks = k_ref[0, pl.ds(pl.multiple_of(c * tk, tk), tk), :]
                vs = vT_ref[0, c, pl.ds(pl.multiple_of(g * HEAD_DIM, HEAD_DIM), HEAD_DIM), :]
                s = jnp.dot(ks, qpad, preferred_element_type=F32)
                m_new = jnp.maximum(m, jnp.max(s, axis=0, keepdims=True))
                alpha = jnp.exp2(m - m_new)
                p = jnp.exp2(s - m_new)
                l = alpha * l + jnp.sum(p, axis=0, keepdims=True)
                acc = alpha * acc + jnp.dot(vs, p.astype(BF16), preferred_element_type=F32)
                return m_new, l, acc

            init = (jnp.full((1, tq), -jnp.inf, F32), jnp.zeros((1, tq), F32), jnp.zeros((HEAD_DIM, tq), F32))
            _, l, acc = lax.fori_loop(0, n_chunks, chunk, init)
            oT_ref[pl.ds(pl.multiple_of(h * HEAD_DIM, HEAD_DIM), HEAD_DIM), :] = acc / l
            return carry

        lax.fori_loop(0, n_heads, per_head, 0)

    att = oT_ref[...].T.astype(BF16)
    mix = jnp.dot(att, wout_ref[0:att_w, :], preferred_element_type=F32)
    mix = mix + jnp.dot(pool_ref[0], wout_ref[att_w:, :], preferred_element_type=F32)
    g1 = mod_ref[0, 2:3, :]
    x1_ref[0] = x_ref[0] + g1 * mix


def _attention(bounded, qT, k, vT, pool, x, mod, w_out, *, tq):
    B, T, D = x.shape
    att_w = qT.shape[1]
    pool_w = pool.shape[2]
    grid_spec = pltpu.PrefetchScalarGridSpec(
        num_scalar_prefetch=1,
        grid=(B, T // tq),
        in_specs=[pl.BlockSpec((1, att_w, tq), lambda b, i, flag: (b, 0, i)),
                  pl.BlockSpec((1,) + k.shape[1:], lambda b, i, flag: (b, 0, 0)),
                  pl.BlockSpec((1,) + vT.shape[1:], lambda b, i, flag: (b, 0, 0, 0)),
                  pl.BlockSpec((1, tq, pool_w), lambda b, i, flag: (b, i, 0)),
                  pl.BlockSpec((1, tq, D), lambda b, i, flag: (b, i, 0)),
                  pl.BlockSpec((1, N_MOD, D), lambda b, i, flag: (b, 0, 0)),
                  pl.BlockSpec(w_out.shape, lambda b, i, flag: (0, 0))],
        out_specs=pl.BlockSpec((1, tq, D), lambda b, i, flag: (b, i, 0)),
        scratch_shapes=[pltpu.VMEM((att_w, tq), F32),
                        pltpu.VMEM((att_w // HEAD_DIM, HEAD_DIM + ONES_ROWS, tq), F32)])
    return pl.pallas_call(
        _attn_kernel,
        grid_spec=grid_spec,
        out_shape=jax.ShapeDtypeStruct((B, T, D), F32),
        compiler_params=pltpu.CompilerParams(vmem_limit_bytes=VMEM_LIMIT),
        name="gqa_outproj",
    )(bounded, qT, k, vT, pool, x, mod, w_out)


def _ffn_kernel(x_ref, xp_ref, xn_ref, mod_ref, g2_ref, wup_ref, cw_ref, cb_ref, wdn_ref, o_ref,
                h_ref, acc_ref, u_ref, y_ref, *, stage_w):
    i = pl.program_id(1)
    n_i = pl.num_programs(1)
    tf = x_ref.shape[1]
    rows = tf + 2 * HALO
    d_ff = wdn_ref.shape[0]

    sh2 = mod_ref[0, 3:4, :]
    sc2 = mod_ref[0, 4:5, :]
    gate2 = mod_ref[0, 5:6, :]
    g2 = g2_ref[...]
    has_prev = (i > 0).astype(F32)
    has_next = (i < n_i - 1).astype(F32)
    h_ref[...] = jnp.concatenate(
        [_rms_modulate(xp_ref[0], g2, sc2, sh2) * has_prev,
         _rms_modulate(x_ref[0], g2, sc2, sh2),
         _rms_modulate(xn_ref[0], g2, sc2, sh2) * has_next], axis=0).astype(BF16)

    half = tf // 2

    def up(slot, s0, w):
        hv = h_ref[...]
        for part, c0 in enumerate((s0, d_ff + s0)):
            u = jnp.dot(hv, wup_ref[:, c0:c0 + w], preferred_element_type=F32)
            for k in range(w // LANES):
                u_ref[slot, part, k] = u[:, k * LANES:(k + 1) * LANES]

    def conv(slot, part, k, c0, parity):
        cs = slice(c0 + k * LANES, c0 + (k + 1) * LANES)
        taps = [u_ref[slot, part, k, pl.ds(HALO - 1 + parity + j, half, stride=2), :] for j in range(CONV_W)]
        y = taps[0] * cw_ref[0:1, cs] + taps[1] * cw_ref[1:2, cs] + taps[2] * cw_ref[2:3, cs]
        return y + cb_ref[:, cs]

    def down(slot, s0, w):
        blocks = []
        for parity in range(2):
            cols = []
            for k in range(w // LANES):
                gate = conv(slot, 0, k, s0, parity)
                val = conv(slot, 1, k, d_ff + s0, parity)
                cols.append((gate * jax.nn.sigmoid(gate) * val).astype(BF16))
            blocks.append(jnp.concatenate(cols, axis=1))
        a = jnp.concatenate(blocks, axis=0)
        part = jnp.dot(a, wdn_ref[s0:s0 + w, :], preferred_element_type=F32)
        if s0 == 0:
            acc_ref[...] = part
        else:
            acc_ref[...] += part

    stages = [(s0, min(stage_w, d_ff - s0)) for s0 in range(0, d_ff, stage_w)]
    pending = None
    for n, (s0, w) in enumerate(stages):
        up(n % 2, s0, w)
        if pending is not None:
            down(*pending)
        pending = (n % 2, s0, w)
    down(*pending)

    for k in range(acc_ref.shape[1] // LANES):
        cs = slice(k * LANES, (k + 1) * LANES)
        for parity in range(2):
            y_ref[k, pl.ds(parity, half, stride=2), :] = acc_ref[parity * half:(parity + 1) * half, cs]
    for k in range(acc_ref.shape[1] // LANES):
        cs = slice(k * LANES, (k + 1) * LANES)
        o_ref[0, :, cs] = x_ref[0, :, cs] + gate2[:, cs] * y_ref[k]


def _ffn(x1, mod, g2, w_up, conv_w, conv_b, w_down, *, tf, stage_w):
    B, T, D = x1.shape
    n_t = T // tf
    d_ff = w_down.shape[0]
    prev_spec, next_spec = _halo_specs(tf, n_t, D)
    const2 = lambda b, i: (0, 0)
    resident = dict(pipeline_mode=pl.Buffered(1))
    kern = functools.partial(_ffn_kernel, stage_w=stage_w)
    return pl.pallas_call(
        kern,
        grid=(B, n_t),
        in_specs=[pl.BlockSpec((1, tf, D), lambda b, i: (b, i, 0)),
                  prev_spec, next_spec,
                  pl.BlockSpec((1, N_MOD, D), lambda b, i: (b, 0, 0)),
                  pl.BlockSpec((1, D), const2),
                  pl.BlockSpec(w_up.shape, const2, **resident),
                  pl.BlockSpec(conv_w.shape, const2),
                  pl.BlockSpec((1, 2 * d_ff), const2),
                  pl.BlockSpec(w_down.shape, const2, **resident)],
        out_specs=pl.BlockSpec((1, tf, D), lambda b, i: (b, i, 0)),
        out_shape=jax.ShapeDtypeStruct((B, T, D), F32),
        scratch_shapes=[pltpu.VMEM((tf + 2 * HALO, D), BF16),
                        pltpu.VMEM((tf, D), F32),
                        pltpu.VMEM((2, 2, stage_w // LANES, tf + 2 * HALO, LANES), F32),
                        pltpu.VMEM((D // LANES, tf, LANES), F32)],
        compiler_params=pltpu.CompilerParams(vmem_limit_bytes=VMEM_LIMIT),
        name="conv_gated_ffn",
    )(x1, x1, x1, mod, g2, w_up, conv_w, conv_b, w_down)


def _rope_tables(T):
    axis_dim = HEAD_DIM // 2
    pos = jnp.arange(T)
    row = (pos // GRID_W).astype(F32)
    col = (pos % GRID_W).astype(F32)
    d = jnp.arange(HEAD_DIM)
    inv = 1.0 / (ROPE_THETA ** (jnp.arange(0, axis_dim, 2, dtype=F32) / axis_dim))
    freq = inv[d % (axis_dim // 2)]
    p = jnp.where((d // axis_dim == 0)[:, None], row[None, :], col[None, :])
    ang = p * freq[:, None]
    sign = jnp.where((d % axis_dim) < axis_dim // 2, -1.0, 1.0).astype(F32)
    return jnp.cos(ang), jnp.sin(ang) * sign[:, None]


def _head_gain_rows(gain):
    return jnp.broadcast_to(jnp.tile(gain, LANES // HEAD_DIM)[:, None], (LANES, LANES))


def _scores_bounded(q_gain, k_gain):
    bound = HEAD_DIM * jnp.max(jnp.abs(q_gain)) * jnp.max(jnp.abs(k_gain))
    return (bound <= SCORE_BOUND).astype(jnp.int32).reshape(1)


def kernel(x, c, w_ada, b_ada, norm1_g, w_in, q_norm_g, k_norm_g, w_pool, pool_scale, w_out, norm2_g,
           w_up, conv_w, conv_b, w_down):
    B, T, D = x.shape
    depth = w_ada.shape[0]
    att_w = D // 2
    tile = 512
    assert T % tile == 0 and T % GRID_W == 0 and D % LANES == 0
    d_ff = w_down.shape[1]
    q_scale = math.log2(math.e) / math.sqrt(HEAD_DIM)
    cos_t, sin_t = _rope_tables(T)

    for l in range(depth):
        mod = _modulation(c, w_ada[l], b_ada[l]).reshape(B, N_MOD, D)
        qT, k, vT, pool = _projection(
            x, mod, norm1_g[l].reshape(1, D), w_in[l].astype(BF16),
            _head_gain_rows(q_norm_g[l] * q_scale), _head_gain_rows(k_norm_g[l]),
            cos_t, sin_t, w_pool[l].astype(BF16), pool_scale[l].reshape(1, -1),
            tt=tile, att_w=att_w)
        bounded = _scores_bounded(q_norm_g[l] * q_scale, k_norm_g[l])
        x1 = _attention(bounded, qT, k, vT, pool, x, mod, w_out[l].astype(BF16), tq=tile)
        x = _ffn(x1, mod, norm2_g[l].reshape(1, D), w_up[l].astype(BF16), conv_w[l],
                 conv_b[l].reshape(1, -1), w_down[l].astype(BF16), tf=tile, stage_w=512)
    return x
```

```python
import functools
import math

import jax
import jax.numpy as jnp
from jax import lax
from jax.experimental import pallas as pl
from jax.experimental.pallas import tpu as pltpu

HEAD_DIM = 64
N_KV_HEADS = 2
GRID_W = 64
ROPE_THETA = 10000.0
POOL_WINDOWS = (2, 4, 8, 16)
CONV_W = 3
N_MOD = 6
EPS = 1e-6

LANES = 128
HALO = 8
VMEM_LIMIT = 56 * 1024 * 1024
SCORE_LOOKAHEAD = 2
KEY_STAGE = 256
SCORE_BOUND = 64.0
DENOM_ROWS = 8

F32 = jnp.float32
BF16 = jnp.bfloat16


def _rms_modulate(xv, g, sc, sh):
    ms = jnp.mean(xv * xv, axis=-1, keepdims=True)
    return xv * lax.rsqrt(ms + EPS) * (g * (1.0 + sc)) + sh


def _halo_specs(tile, n_tiles, width):
    per = tile // HALO
    last = n_tiles * per - 1
    prev = pl.BlockSpec((1, HALO, width), lambda b, i: (b, jnp.maximum(i * per - 1, 0), 0))
    nxt = pl.BlockSpec((1, HALO, width), lambda b, i: (b, jnp.minimum((i + 1) * per, last), 0))
    return prev, nxt


def _mod_kernel(c_ref, w_ref, b_ref, o_ref):
    c = c_ref[...]
    c_act = c * jax.nn.sigmoid(c)
    o_ref[...] = jnp.dot(c_act, w_ref[...], preferred_element_type=F32) + b_ref[...]


def _modulation(c, w_ada, b_ada):
    B, D = c.shape
    n = w_ada.shape[1]
    bn = D
    return pl.pallas_call(
        _mod_kernel,
        grid=(n // bn,),
        in_specs=[pl.BlockSpec((B, D), lambda j: (0, 0)),
                  pl.BlockSpec((D, bn), lambda j: (0, j)),
                  pl.BlockSpec((1, bn), lambda j: (0, j))],
        out_specs=pl.BlockSpec((B, bn), lambda j: (0, j)),
        out_shape=jax.ShapeDtypeStruct((B, n), F32),
        compiler_params=pltpu.CompilerParams(vmem_limit_bytes=VMEM_LIMIT),
        name="adaln_modulation",
    )(c, w_ada, b_ada.reshape(1, n))


def _proj_kernel(x_ref, xp_ref, xn_ref, mod_ref, g1_ref, win_ref, gq_ref, gk_ref, cos_ref, sin_ref,
                 wpool_ref, pscale_ref, qT_ref, k_ref, vT_ref, pool_ref, u_ref, y_ref, *, seq_len):
    i = pl.program_id(1)
    n_i = pl.num_programs(1)
    tt = x_ref.shape[1]
    half = tt // 2
    att_w = qT_ref.shape[1]
    kv_w = N_KV_HEADS * HEAD_DIM
    pool_w = pool_ref.shape[2]
    pool0 = att_w + 2 * kv_w

    sh1 = mod_ref[0, 0:1, :]
    sc1 = mod_ref[0, 1:2, :]
    g1 = g1_ref[...]
    h = _rms_modulate(x_ref[0], g1, sc1, sh1).astype(BF16)
    has_prev = (i > 0).astype(F32)
    has_next = (i < n_i - 1).astype(F32)
    h_halo = jnp.concatenate([_rms_modulate(xp_ref[0], g1, sc1, sh1) * has_prev,
                              _rms_modulate(xn_ref[0], g1, sc1, sh1) * has_next], axis=0).astype(BF16)

    w_pool_in = win_ref[:, pool0:pool0 + pool_w]
    u_main = jnp.dot(h, w_pool_in, preferred_element_type=F32)
    u_halo = jnp.dot(h_halo, w_pool_in, preferred_element_type=F32)
    qkv = jnp.dot(h, win_ref[:, 0:pool0], preferred_element_type=F32)
    gdim = pool_w // len(POOL_WINDOWS)
    for g in range(len(POOL_WINDOWS)):
        cs = slice(g * gdim, (g + 1) * gdim)
        u_ref[g, 0:HALO, :] = u_halo[0:HALO, cs]
        u_ref[g, HALO:HALO + tt, :] = u_main[:, cs]
        u_ref[g, HALO + tt:, :] = u_halo[HALO:, cs]

    j2 = 2 * lax.broadcasted_iota(jnp.int32, (half, 1), 0)
    for g, w in enumerate(POOL_WINDOWS):
        cs = slice(g * gdim, (g + 1) * gdim)
        halves = []
        for parity in range(2):
            taps = [u_ref[g, pl.ds(HALO + o + parity, half, stride=2), :] for o in range(-w // 2, w // 2)]
            win = taps[0]
            for tap in taps[1:]:
                win = win + tap
            t = i * tt + j2 + parity
            lo = jnp.maximum(t - w // 2, 0)
            hi = jnp.minimum(t + w // 2 - 1, seq_len - 1)
            cnt = (hi - lo + 1).astype(F32)
            halves.append(win / cnt - taps[w // 2])
        pooled = jnp.concatenate(halves, axis=0).astype(BF16)
        mixed = jnp.dot(pooled, wpool_ref[g], preferred_element_type=F32) * pscale_ref[:, cs]
        for parity in range(2):
            y_ref[g, pl.ds(parity, half, stride=2), :] = mixed[parity * half:(parity + 1) * half]
        pool_ref[0, :, cs] = y_ref[g].astype(BF16)

    cos = cos_ref[...]
    sin = sin_ref[...]
    n_col = tt // LANES

    def norm_rope_t(blk, gain_ref):
        xt = blk.T
        gain = jnp.concatenate([gain_ref[...]] * n_col, axis=1)
        outs = []
        for hh in range(LANES // HEAD_DIM):
            rows = slice(hh * HEAD_DIM, (hh + 1) * HEAD_DIM)
            xh = xt[rows, :]
            ms = jnp.mean(xh * xh, axis=0, keepdims=True)
            y = xh * lax.rsqrt(ms + EPS) * gain[rows, :]
            q4 = HEAD_DIM // 4
            partner = jnp.concatenate([y[q4:2 * q4], y[0:q4], y[3 * q4:], y[2 * q4:3 * q4]], axis=0)
            outs.append(y * cos + partner * sin)
        return outs

    for blk in range(att_w // LANES):
        for hh, qt in enumerate(norm_rope_t(qkv[:, blk * LANES:(blk + 1) * LANES], gq_ref)):
            h = blk * (LANES // HEAD_DIM) + hh
            qT_ref[0, h * HEAD_DIM:(h + 1) * HEAD_DIM, :] = qt.astype(BF16)

    k_heads = norm_rope_t(qkv[:, att_w:att_w + kv_w], gk_ref)
    k_ref[0] = jnp.concatenate(k_heads, axis=0).T.astype(BF16)

    vT_ref[0, 0] = qkv[:, att_w + kv_w:pool0].T.astype(BF16)


def _projection(x, mod, g1, w_in, gq, gk, cos_t, sin_t, w_pool, pool_scale, *, tt, att_w):
    B, T, D = x.shape
    n_t = T // tt
    in_w = w_in.shape[1]
    kv_w = N_KV_HEADS * HEAD_DIM
    pool_w = in_w - att_w - 2 * kv_w
    prev_spec, next_spec = _halo_specs(tt, n_t, D)
    const2 = lambda b, i: (0, 0)
    kern = functools.partial(_proj_kernel, seq_len=T)
    return pl.pallas_call(
        kern,
        grid=(B, n_t),
        in_specs=[pl.BlockSpec((1, tt, D), lambda b, i: (b, i, 0)),
                  prev_spec, next_spec,
                  pl.BlockSpec((1, N_MOD, D), lambda b, i: (b, 0, 0)),
                  pl.BlockSpec((1, D), const2),
                  pl.BlockSpec((D, in_w), const2),
                  pl.BlockSpec((LANES, LANES), const2),
                  pl.BlockSpec((LANES, LANES), const2),
                  pl.BlockSpec((HEAD_DIM, tt), lambda b, i: (0, i)),
                  pl.BlockSpec((HEAD_DIM, tt), lambda b, i: (0, i)),
                  pl.BlockSpec(w_pool.shape, lambda b, i: (0, 0, 0)),
                  pl.BlockSpec((1, pool_w), const2)],
        out_specs=[pl.BlockSpec((1, att_w, tt), lambda b, i: (b, 0, i)),
                   pl.BlockSpec((1, tt, kv_w), lambda b, i: (b, i, 0)),
                   pl.BlockSpec((1, 1, kv_w, tt), lambda b, i: (b, i, 0, 0)),
                   pl.BlockSpec((1, tt, pool_w), lambda b, i: (b, i, 0))],
        out_shape=[jax.ShapeDtypeStruct((B, att_w, T), BF16),
                   jax.ShapeDtypeStruct((B, T, kv_w), BF16),
                   jax.ShapeDtypeStruct((B, n_t, kv_w, tt), BF16),
                   jax.ShapeDtypeStruct((B, T, pool_w), BF16)],
        scratch_shapes=[pltpu.VMEM((len(POOL_WINDOWS), tt + 2 * HALO, LANES), F32),
                        pltpu.VMEM((len(POOL_WINDOWS), tt, LANES), F32)],
        compiler_params=pltpu.CompilerParams(vmem_limit_bytes=VMEM_LIMIT),
        name="proj_qkv_pool",
    )(x, x, x, mod, g1, w_in, gq, gk, cos_t, sin_t, w_pool, pool_scale)


def _attn_kernel(bounded_ref, qT_ref, k_ref, vT_ref, pool_ref, x_ref, mod_ref, wout_ref, x1_ref,
                 oT_ref, acc_ref):
    n_chunks = vT_ref.shape[1]
    tk = vT_ref.shape[3]
    tq = qT_ref.shape[2]
    att_w = qT_ref.shape[1]
    n_heads = att_w // HEAD_DIM
    q_per_kv = n_heads // N_KV_HEADS

    def head_rows(h):
        return slice(h * HEAD_DIM, (h + 1) * HEAD_DIM)

    def padded_q(qh, g):
        zero = jnp.zeros_like(qh)
        return jnp.concatenate([qh if gg == g else zero for gg in range(N_KV_HEADS)], axis=0)

    bounded = bounded_ref[0] == 1

    @pl.when(bounded)
    def _():
        per_chunk = tk // KEY_STAGE

        def scores(c, h):
            ks = k_ref[0, c * KEY_STAGE:(c + 1) * KEY_STAGE, :]
            return jnp.dot(ks, padded_q(qT_ref[0, head_rows(h), :], h // q_per_kv),
                           preferred_element_type=F32)

        def accumulate(c, h, s):
            g = h // q_per_kv
            lanes = slice((c % per_chunk) * KEY_STAGE, (c % per_chunk + 1) * KEY_STAGE)
            p = jnp.exp2(s)
            l = jnp.sum(p, axis=0, keepdims=True)
            pv = jnp.dot(vT_ref[0, c // per_chunk, head_rows(g), lanes], p.astype(BF16), preferred_element_type=F32)
            if c == 0:
                acc_ref[h, 0:HEAD_DIM, :] = pv
                acc_ref[h, HEAD_DIM:HEAD_DIM + 1, :] = l
            else:
                acc_ref[h, 0:HEAD_DIM, :] += pv
                acc_ref[h, HEAD_DIM:HEAD_DIM + 1, :] += l

        stages = [(c, g * q_per_kv + hh) for g in range(N_KV_HEADS) for c in range(n_chunks * per_chunk)
                  for hh in range(q_per_kv)]
        in_flight = []
        for c, h in stages:
            in_flight.append((c, h, scores(c, h)))
            if len(in_flight) > SCORE_LOOKAHEAD:
                accumulate(*in_flight.pop(0))
        for item in in_flight:
            accumulate(*item)
        for h in range(n_heads):
            oT_ref[head_rows(h), :] = acc_ref[h, 0:HEAD_DIM, :] / acc_ref[h, HEAD_DIM:HEAD_DIM + 1, :]

    @pl.when(jnp.logical_not(bounded))
    def _():
        def per_head(h, carry):
            g = h // q_per_kv
            qh = qT_ref[0, pl.ds(pl.multiple_of(h * HEAD_DIM, HEAD_DIM), HEAD_DIM), :]
            zero = jnp.zeros_like(qh)
            qpad = jnp.concatenate([jnp.where(g == gg, qh, zero) for gg in range(N_KV_HEADS)], axis=0)

            def chunk(c, state):
                m, l, acc = state
                ks = k_ref[0, pl.ds(pl.multiple_of(c * tk, tk), tk), :]
                vs = vT_ref[0, c, pl.ds(pl.multiple_of(g * HEAD_DIM, HEAD_DIM), HEAD_DIM), :]
                s = jnp.dot(ks, qpad, preferred_element_type=F32)
                m_new = jnp.maximum(m, jnp.max(s, axis=0, keepdims=True))
                alpha = jnp.exp2(m - m_new)
                p = jnp.exp2(s - m_new)
                l = alpha * l + jnp.sum(p, axis=0, keepdims=True)
                acc = alpha * acc + jnp.dot(vs, p.astype(BF16), preferred_element_type=F32)
                return m_new, l, acc

            init = (jnp.full((1, tq), -jnp.inf, F32), jnp.zeros((1, tq), F32), jnp.zeros((HEAD_DIM, tq), F32))
            _, l, acc = lax.fori_loop(0, n_chunks, chunk, init)
            oT_ref[pl.ds(pl.multiple_of(h * HEAD_DIM, HEAD_DIM), HEAD_DIM), :] = acc / l
            return carry

        lax.fori_loop(0, n_heads, per_head, 0)

    att = oT_ref[...].T.astype(BF16)
    mix = jnp.dot(att, wout_ref[0:att_w, :], preferred_element_type=F32)
    mix = mix + jnp.dot(pool_ref[0], wout_ref[att_w:, :], preferred_element_type=F32)
    g1 = mod_ref[0, 2:3, :]
    x1_ref[0] = x_ref[0] + g1 * mix


def _attention(bounded, qT, k, vT, pool, x, mod, w_out, *, tq):
    B, T, D = x.shape
    att_w = qT.shape[1]
    pool_w = pool.shape[2]
    grid_spec = pltpu.PrefetchScalarGridSpec(
        num_scalar_prefetch=1,
        grid=(B, T // tq),
        in_specs=[pl.BlockSpec((1, att_w, tq), lambda b, i, flag: (b, 0, i)),
                  pl.BlockSpec((1,) + k.shape[1:], lambda b, i, flag: (b, 0, 0)),
                  pl.BlockSpec((1,) + vT.shape[1:], lambda b, i, flag: (b, 0, 0, 0)),
                  pl.BlockSpec((1, tq, pool_w), lambda b, i, flag: (b, i, 0)),
                  pl.BlockSpec((1, tq, D), lambda b, i, flag: (b, i, 0)),
                  pl.BlockSpec((1, N_MOD, D), lambda b, i, flag: (b, 0, 0)),
                  pl.BlockSpec(w_out.shape, lambda b, i, flag: (0, 0))],
        out_specs=pl.BlockSpec((1, tq, D), lambda b, i, flag: (b, i, 0)),
        scratch_shapes=[pltpu.VMEM((att_w, tq), F32),
                        pltpu.VMEM((att_w // HEAD_DIM, HEAD_DIM + DENOM_ROWS, tq), F32)])
    return pl.pallas_call(
        _attn_kernel,
        grid_spec=grid_spec,
        out_shape=jax.ShapeDtypeStruct((B, T, D), F32),
        compiler_params=pltpu.CompilerParams(vmem_limit_bytes=VMEM_LIMIT),
        name="gqa_outproj",
    )(bounded, qT, k, vT, pool, x, mod, w_out)


def _ffn_kernel(x_ref, xp_ref, xn_ref, mod_ref, g2_ref, wup_ref, cw_ref, cb_ref, wdn_ref, o_ref,
                h_ref, acc_ref, u_ref, y_ref, *, stage_w):
    i = pl.program_id(1)
    n_i = pl.num_programs(1)
    tf = x_ref.shape[1]
    rows = tf + 2 * HALO
    d_ff = wdn_ref.shape[0]

    sh2 = mod_ref[0, 3:4, :]
    sc2 = mod_ref[0, 4:5, :]
    gate2 = mod_ref[0, 5:6, :]
    g2 = g2_ref[...]
    has_prev = (i > 0).astype(F32)
    has_next = (i < n_i - 1).astype(F32)
    h_ref[...] = jnp.concatenate(
        [_rms_modulate(xp_ref[0], g2, sc2, sh2) * has_prev,
         _rms_modulate(x_ref[0], g2, sc2, sh2),
         _rms_modulate(xn_ref[0], g2, sc2, sh2) * has_next], axis=0).astype(BF16)

    half = tf // 2

    def up(slot, s0, w):
        hv = h_ref[...]
        for part, c0 in enumerate((s0, d_ff + s0)):
            u = jnp.dot(hv, wup_ref[:, c0:c0 + w], preferred_element_type=F32)
            for k in range(w // LANES):
                u_ref[slot, part, k] = u[:, k * LANES:(k + 1) * LANES]

    def conv(slot, part, k, c0, parity):
        cs = slice(c0 + k * LANES, c0 + (k + 1) * LANES)
        taps = [u_ref[slot, part, k, pl.ds(HALO - 1 + parity + j, half, stride=2), :] for j in range(CONV_W)]
        y = taps[0] * cw_ref[0:1, cs] + taps[1] * cw_ref[1:2, cs] + taps[2] * cw_ref[2:3, cs]
        return y + cb_ref[:, cs]

    def down(slot, s0, w):
        blocks = []
        for parity in range(2):
            cols = []
            for k in range(w // LANES):
                gate = conv(slot, 0, k, s0, parity)
                val = conv(slot, 1, k, d_ff + s0, parity)
                cols.append((gate * jax.nn.sigmoid(gate) * val).astype(BF16))
            blocks.append(jnp.concatenate(cols, axis=1))
        a = jnp.concatenate(blocks, axis=0)
        part = jnp.dot(a, wdn_ref[s0:s0 + w, :], preferred_element_type=F32)
        if s0 == 0:
            acc_ref[...] = part
        else:
            acc_ref[...] += part

    stages = [(s0, min(stage_w, d_ff - s0)) for s0 in range(0, d_ff, stage_w)]
    pending = None
    for n, (s0, w) in enumerate(stages):
        up(n % 2, s0, w)
        if pending is not None:
            down(*pending)
        pending = (n % 2, s0, w)
    down(*pending)

    for k in range(acc_ref.shape[1] // LANES):
        cs = slice(k * LANES, (k + 1) * LANES)
        for parity in range(2):
            y_ref[k, pl.ds(parity, half, stride=2), :] = acc_ref[parity * half:(parity + 1) * half, cs]
    for k in range(acc_ref.shape[1] // LANES):
        cs = slice(k * LANES, (k + 1) * LANES)
        o_ref[0, :, cs] = x_ref[0, :, cs] + gate2[:, cs] * y_ref[k]


def _ffn(x1, mod, g2, w_up, conv_w, conv_b, w_down, *, tf, stage_w):
    B, T, D = x1.shape
    n_t = T // tf
    d_ff = w_down.shape[0]
    prev_spec, next_spec = _halo_specs(tf, n_t, D)
    const2 = lambda b, i: (0, 0)
    resident = dict(pipeline_mode=pl.Buffered(1))
    kern = functools.partial(_ffn_kernel, stage_w=stage_w)
    return pl.pallas_call(
        kern,
        grid=(B, n_t),
        in_specs=[pl.BlockSpec((1, tf, D), lambda b, i: (b, i, 0)),
                  prev_spec, next_spec,
                  pl.BlockSpec((1, N_MOD, D), lambda b, i: (b, 0, 0)),
                  pl.BlockSpec((1, D), const2),
                  pl.BlockSpec(w_up.shape, const2, **resident),
                  pl.BlockSpec(conv_w.shape, const2),
                  pl.BlockSpec((1, 2 * d_ff), const2),
                  pl.BlockSpec(w_down.shape, const2, **resident)],
        out_specs=pl.BlockSpec((1, tf, D), lambda b, i: (b, i, 0)),
        out_shape=jax.ShapeDtypeStruct((B, T, D), F32),
        scratch_shapes=[pltpu.VMEM((tf + 2 * HALO, D), BF16),
                        pltpu.VMEM((tf, D), F32),
                        pltpu.VMEM((2, 2, stage_w // LANES, tf + 2 * HALO, LANES), F32),
                        pltpu.VMEM((D // LANES, tf, LANES), F32)],
        compiler_params=pltpu.CompilerParams(vmem_limit_bytes=VMEM_LIMIT),
        name="conv_gated_ffn",
    )(x1, x1, x1, mod, g2, w_up, conv_w, conv_b, w_down)


def _rope_tables(T):
    axis_dim = HEAD_DIM // 2
    pos = jnp.arange(T)
    row = (pos // GRID_W).astype(F32)
    col = (pos % GRID_W).astype(F32)
    d = jnp.arange(HEAD_DIM)
    inv = 1.0 / (ROPE_THETA ** (jnp.arange(0, axis_dim, 2, dtype=F32) / axis_dim))
    freq = inv[d % (axis_dim // 2)]
    p = jnp.where((d // axis_dim == 0)[:, None], row[None, :], col[None, :])
    ang = p * freq[:, None]
    sign = jnp.where((d % axis_dim) < axis_dim // 2, -1.0, 1.0).astype(F32)
    return jnp.cos(ang), jnp.sin(ang) * sign[:, None]


def _head_gain_rows(gain):
    return jnp.broadcast_to(jnp.tile(gain, LANES // HEAD_DIM)[:, None], (LANES, LANES))


def _scores_bounded(q_gain, k_gain):
    bound = HEAD_DIM * jnp.max(jnp.abs(q_gain)) * jnp.max(jnp.abs(k_gain))
    return (bound <= SCORE_BOUND).astype(jnp.int32).reshape(1)


def kernel(x, c, w_ada, b_ada, norm1_g, w_in, q_norm_g, k_norm_g, w_pool, pool_scale, w_out, norm2_g,
           w_up, conv_w, conv_b, w_down):
    B, T, D = x.shape
    depth = w_ada.shape[0]
    att_w = D // 2
    tile = 512
    assert T % tile == 0 and T % GRID_W == 0 and D % LANES == 0
    d_ff = w_down.shape[1]
    q_scale = math.log2(math.e) / math.sqrt(HEAD_DIM)
    cos_t, sin_t = _rope_tables(T)

    for l in range(depth):
        mod = _modulation(c, w_ada[l], b_ada[l]).reshape(B, N_MOD, D)
        qT, k, vT, pool = _projection(
            x, mod, norm1_g[l].reshape(1, D), w_in[l].astype(BF16),
            _head_gain_rows(q_norm_g[l] * q_scale), _head_gain_rows(k_norm_g[l]),
            cos_t, sin_t, w_pool[l].astype(BF16), pool_scale[l].reshape(1, -1),
            tt=tile, att_w=att_w)
        bounded = _scores_bounded(q_norm_g[l] * q_scale, k_norm_g[l])
        x1 = _attention(bounded, qT, k, vT, pool, x, mod, w_out[l].astype(BF16), tq=tile)
        x = _ffn(x1, mod, norm2_g[l].reshape(1, D), w_up[l].astype(BF16), conv_w[l],
                 conv_b[l].reshape(1, -1), w_down[l].astype(BF16), tf=tile, stage_w=512)
    return x
```

```python
import functools
import math

import numpy as np
import jax
import jax.numpy as jnp
from jax import lax
from jax.experimental import pallas as pl
from jax.experimental.pallas import tpu as pltpu

HEAD_DIM = 64
N_KV_HEADS = 2
GRID_W = 64
ROPE_THETA = 10000.0
POOL_WINDOWS = (2, 4, 8, 16)
CONV_W = 3
N_MOD = 6
EPS = 1e-6

LANES = 128
HALO = 8
VMEM_LIMIT = 56 * 1024 * 1024
TOKEN_TILE = 512
FFN_STAGE = 512
SCORE_LOOKAHEAD = 2
KEY_STAGE = 256
SCORE_BOUND = 64.0
DENOM_ROWS = 8

F32 = jnp.float32
BF16 = jnp.bfloat16


def _rms_modulate(xv, g, sc, sh):
    ms = jnp.mean(xv * xv, axis=-1, keepdims=True)
    return xv * lax.rsqrt(ms + EPS) * (g * (1.0 + sc)) + sh


def _halo_specs(tile, n_tiles, width):
    per = tile // HALO
    last = n_tiles * per - 1
    prev = pl.BlockSpec((1, HALO, width), lambda b, i: (b, jnp.maximum(i * per - 1, 0), 0))
    nxt = pl.BlockSpec((1, HALO, width), lambda b, i: (b, jnp.minimum((i + 1) * per, last), 0))
    return prev, nxt


def _mod_kernel(c_ref, w_ref, b_ref, o_ref):
    c = c_ref[...]
    c_act = c * jax.nn.sigmoid(c)
    o_ref[...] = jnp.dot(c_act, w_ref[...], preferred_element_type=F32) + b_ref[...]


def _modulation(c, w_ada, b_ada):
    B, D = c.shape
    n = w_ada.shape[1]
    bn = D
    return pl.pallas_call(
        _mod_kernel,
        grid=(n // bn,),
        in_specs=[pl.BlockSpec((B, D), lambda j: (0, 0)),
                  pl.BlockSpec((D, bn), lambda j: (0, j)),
                  pl.BlockSpec((1, bn), lambda j: (0, j))],
        out_specs=pl.BlockSpec((B, bn), lambda j: (0, j)),
        out_shape=jax.ShapeDtypeStruct((B, n), F32),
        compiler_params=pltpu.CompilerParams(vmem_limit_bytes=VMEM_LIMIT),
        name="adaln_modulation",
    )(c, w_ada, b_ada.reshape(1, n))


def _proj_kernel(x_ref, xp_ref, xn_ref, mod_ref, g1_ref, win_ref, gq_ref, gk_ref, cos_ref, sin_ref,
                 wpool_ref, pscale_ref, qT_ref, k_ref, vT_ref, pool_ref, u_ref, y_ref, *, seq_len):
    i = pl.program_id(1)
    n_i = pl.num_programs(1)
    tt = x_ref.shape[1]
    half = tt // 2
    att_w = qT_ref.shape[1]
    kv_w = N_KV_HEADS * HEAD_DIM
    pool_w = pool_ref.shape[2]
    pool0 = att_w + 2 * kv_w

    sh1 = mod_ref[0, 0:1, :]
    sc1 = mod_ref[0, 1:2, :]
    g1 = g1_ref[...]
    h = _rms_modulate(x_ref[0], g1, sc1, sh1).astype(BF16)
    has_prev = (i > 0).astype(F32)
    has_next = (i < n_i - 1).astype(F32)
    h_halo = jnp.concatenate([_rms_modulate(xp_ref[0], g1, sc1, sh1) * has_prev,
                              _rms_modulate(xn_ref[0], g1, sc1, sh1) * has_next], axis=0).astype(BF16)

    w_pool_in = win_ref[:, pool0:pool0 + pool_w]
    u_main = jnp.dot(h, w_pool_in, preferred_element_type=F32)
    u_halo = jnp.dot(h_halo, w_pool_in, preferred_element_type=F32)
    qkv = jnp.dot(h, win_ref[:, 0:pool0], preferred_element_type=F32)
    gdim = pool_w // len(POOL_WINDOWS)
    for g in range(len(POOL_WINDOWS)):
        cs = slice(g * gdim, (g + 1) * gdim)
        u_ref[g, 0:HALO, :] = u_halo[0:HALO, cs]
        u_ref[g, HALO:HALO + tt, :] = u_main[:, cs]
        u_ref[g, HALO + tt:, :] = u_halo[HALO:, cs]

    j2 = 2 * lax.broadcasted_iota(jnp.int32, (half, 1), 0)
    for g, w in enumerate(POOL_WINDOWS):
        cs = slice(g * gdim, (g + 1) * gdim)
        halves = []
        for parity in range(2):
            taps = [u_ref[g, pl.ds(HALO + o + parity, half, stride=2), :] for o in range(-w // 2, w // 2)]
            win = taps[0]
            for tap in taps[1:]:
                win = win + tap
            t = i * tt + j2 + parity
            lo = jnp.maximum(t - w // 2, 0)
            hi = jnp.minimum(t + w // 2 - 1, seq_len - 1)
            cnt = (hi - lo + 1).astype(F32)
            halves.append(win / cnt - taps[w // 2])
        pooled = jnp.concatenate(halves, axis=0).astype(BF16)
        mixed = jnp.dot(pooled, wpool_ref[g], preferred_element_type=F32) * pscale_ref[:, cs]
        for parity in range(2):
            y_ref[g, pl.ds(parity, half, stride=2), :] = mixed[parity * half:(parity + 1) * half]
        pool_ref[0, :, cs] = y_ref[g].astype(BF16)

    cos = cos_ref[...]
    sin = sin_ref[...]
    n_col = tt // LANES

    def norm_rope_t(blk, gain_ref):
        xt = blk.T
        gain = jnp.concatenate([gain_ref[...]] * n_col, axis=1)
        outs = []
        for hh in range(LANES // HEAD_DIM):
            rows = slice(hh * HEAD_DIM, (hh + 1) * HEAD_DIM)
            xh = xt[rows, :]
            ms = jnp.mean(xh * xh, axis=0, keepdims=True)
            y = xh * lax.rsqrt(ms + EPS) * gain[rows, :]
            q4 = HEAD_DIM // 4
            partner = jnp.concatenate([y[q4:2 * q4], y[0:q4], y[3 * q4:], y[2 * q4:3 * q4]], axis=0)
            outs.append(y * cos + partner * sin)
        return outs

    for blk in range(att_w // LANES):
        for hh, qt in enumerate(norm_rope_t(qkv[:, blk * LANES:(blk + 1) * LANES], gq_ref)):
            h = blk * (LANES // HEAD_DIM) + hh
            qT_ref[0, h * HEAD_DIM:(h + 1) * HEAD_DIM, :] = qt.astype(BF16)

    k_heads = norm_rope_t(qkv[:, att_w:att_w + kv_w], gk_ref)
    k_ref[0] = jnp.concatenate(k_heads, axis=0).T.astype(BF16)

    vT_ref[0, 0] = qkv[:, att_w + kv_w:pool0].T.astype(BF16)


def _projection(x, mod, g1, w_in, gq, gk, cos_t, sin_t, w_pool, pool_scale, *, tt, att_w):
    B, T, D = x.shape
    n_t = T // tt
    in_w = w_in.shape[1]
    kv_w = N_KV_HEADS * HEAD_DIM
    pool_w = in_w - att_w - 2 * kv_w
    prev_spec, next_spec = _halo_specs(tt, n_t, D)
    const2 = lambda b, i: (0, 0)
    kern = functools.partial(_proj_kernel, seq_len=T)
    return pl.pallas_call(
        kern,
        grid=(B, n_t),
        in_specs=[pl.BlockSpec((1, tt, D), lambda b, i: (b, i, 0)),
                  prev_spec, next_spec,
                  pl.BlockSpec((1, N_MOD, D), lambda b, i: (b, 0, 0)),
                  pl.BlockSpec((1, D), const2),
                  pl.BlockSpec((D, in_w), const2),
                  pl.BlockSpec((LANES, LANES), const2),
                  pl.BlockSpec((LANES, LANES), const2),
                  pl.BlockSpec((HEAD_DIM, tt), lambda b, i: (0, i)),
                  pl.BlockSpec((HEAD_DIM, tt), lambda b, i: (0, i)),
                  pl.BlockSpec(w_pool.shape, lambda b, i: (0, 0, 0)),
                  pl.BlockSpec((1, pool_w), const2)],
        out_specs=[pl.BlockSpec((1, att_w, tt), lambda b, i: (b, 0, i)),
                   pl.BlockSpec((1, tt, kv_w), lambda b, i: (b, i, 0)),
                   pl.BlockSpec((1, 1, kv_w, tt), lambda b, i: (b, i, 0, 0)),
                   pl.BlockSpec((1, tt, pool_w), lambda b, i: (b, i, 0))],
        out_shape=[jax.ShapeDtypeStruct((B, att_w, T), BF16),
                   jax.ShapeDtypeStruct((B, T, kv_w), BF16),
                   jax.ShapeDtypeStruct((B, n_t, kv_w, tt), BF16),
                   jax.ShapeDtypeStruct((B, T, pool_w), BF16)],
        scratch_shapes=[pltpu.VMEM((len(POOL_WINDOWS), tt + 2 * HALO, LANES), F32),
                        pltpu.VMEM((len(POOL_WINDOWS), tt, LANES), F32)],
        compiler_params=pltpu.CompilerParams(vmem_limit_bytes=VMEM_LIMIT),
        name="proj_qkv_pool",
    )(x, x, x, mod, g1, w_in, gq, gk, cos_t, sin_t, w_pool, pool_scale)


def _attn_kernel(bounded_ref, qT_ref, k_ref, vT_ref, pool_ref, x_ref, mod_ref, wout_ref, x1_ref,
                 oT_ref, acc_ref):
    n_chunks = vT_ref.shape[1]
    tk = vT_ref.shape[3]
    tq = qT_ref.shape[2]
    att_w = qT_ref.shape[1]
    n_heads = att_w // HEAD_DIM
    q_per_kv = n_heads // N_KV_HEADS

    def head_rows(h):
        return slice(h * HEAD_DIM, (h + 1) * HEAD_DIM)

    def padded_q(qh, g):
        zero = jnp.zeros_like(qh)
        return jnp.concatenate([qh if gg == g else zero for gg in range(N_KV_HEADS)], axis=0)

    bounded = bounded_ref[0] == 1

    @pl.when(bounded)
    def _():
        per_chunk = tk // KEY_STAGE

        def scores(c, h):
            ks = k_ref[0, c * KEY_STAGE:(c + 1) * KEY_STAGE, :]
            return jnp.dot(ks, padded_q(qT_ref[0, head_rows(h), :], h // q_per_kv),
                           preferred_element_type=F32)

        def accumulate(c, h, s):
            g = h // q_per_kv
            lanes = slice((c % per_chunk) * KEY_STAGE, (c % per_chunk + 1) * KEY_STAGE)
            p = jnp.exp2(s)
            l = jnp.sum(p, axis=0, keepdims=True)
            pv = jnp.dot(vT_ref[0, c // per_chunk, head_rows(g), lanes], p.astype(BF16), preferred_element_type=F32)
            if c == 0:
                acc_ref[h, 0:HEAD_DIM, :] = pv
                acc_ref[h, HEAD_DIM:HEAD_DIM + 1, :] = l
            else:
                acc_ref[h, 0:HEAD_DIM, :] += pv
                acc_ref[h, HEAD_DIM:HEAD_DIM + 1, :] += l

        stages = [(c, g * q_per_kv + hh) for g in range(N_KV_HEADS) for c in range(n_chunks * per_chunk)
                  for hh in range(q_per_kv)]
        in_flight = []
        for c, h in stages:
            in_flight.append((c, h, scores(c, h)))
            if len(in_flight) > SCORE_LOOKAHEAD:
                accumulate(*in_flight.pop(0))
        for item in in_flight:
            accumulate(*item)
        for h in range(n_heads):
            oT_ref[head_rows(h), :] = acc_ref[h, 0:HEAD_DIM, :] / acc_ref[h, HEAD_DIM:HEAD_DIM + 1, :]

    @pl.when(jnp.logical_not(bounded))
    def _():
        def per_head(h, carry):
            g = h // q_per_kv
            qh = qT_ref[0, pl.ds(pl.multiple_of(h * HEAD_DIM, HEAD_DIM), HEAD_DIM), :]
            zero = jnp.zeros_like(qh)
            qpad = jnp.concatenate([jnp.where(g == gg, qh, zero) for gg in range(N_KV_HEADS)], axis=0)

            def chunk(c, state):
                m, l, acc = state
                ks = k_ref[0, pl.ds(pl.multiple_of(c * tk, tk), tk), :]
                vs = vT_ref[0, c, pl.ds(pl.multiple_of(g * HEAD_DIM, HEAD_DIM), HEAD_DIM), :]
                s = jnp.dot(ks, qpad, preferred_element_type=F32)
                m_new = jnp.maximum(m, jnp.max(s, axis=0, keepdims=True))
                alpha = jnp.exp2(m - m_new)
                p = jnp.exp2(s - m_new)
                l = alpha * l + jnp.sum(p, axis=0, keepdims=True)
                acc = alpha * acc + jnp.dot(vs, p.astype(BF16), preferred_element_type=F32)
                return m_new, l, acc

            init = (jnp.full((1, tq), -jnp.inf, F32), jnp.zeros((1, tq), F32), jnp.zeros((HEAD_DIM, tq), F32))
            _, l, acc = lax.fori_loop(0, n_chunks, chunk, init)
            oT_ref[pl.ds(pl.multiple_of(h * HEAD_DIM, HEAD_DIM), HEAD_DIM), :] = acc / l
            return carry

        lax.fori_loop(0, n_heads, per_head, 0)

    att = oT_ref[...].T.astype(BF16)
    mix = jnp.dot(att, wout_ref[0:att_w, :], preferred_element_type=F32)
    mix = mix + jnp.dot(pool_ref[0], wout_ref[att_w:, :], preferred_element_type=F32)
    g1 = mod_ref[0, 2:3, :]
    x1_ref[0] = x_ref[0] + g1 * mix


def _attention(bounded, qT, k, vT, pool, x, mod, w_out, *, tq):
    B, T, D = x.shape
    att_w = qT.shape[1]
    pool_w = pool.shape[2]
    grid_spec = pltpu.PrefetchScalarGridSpec(
        num_scalar_prefetch=1,
        grid=(B, T // tq),
        in_specs=[pl.BlockSpec((1, att_w, tq), lambda b, i, flag: (b, 0, i)),
                  pl.BlockSpec((1,) + k.shape[1:], lambda b, i, flag: (b, 0, 0)),
                  pl.BlockSpec((1,) + vT.shape[1:], lambda b, i, flag: (b, 0, 0, 0)),
                  pl.BlockSpec((1, tq, pool_w), lambda b, i, flag: (b, i, 0)),
                  pl.BlockSpec((1, tq, D), lambda b, i, flag: (b, i, 0)),
                  pl.BlockSpec((1, N_MOD, D), lambda b, i, flag: (b, 0, 0)),
                  pl.BlockSpec(w_out.shape, lambda b, i, flag: (0, 0))],
        out_specs=pl.BlockSpec((1, tq, D), lambda b, i, flag: (b, i, 0)),
        scratch_shapes=[pltpu.VMEM((att_w, tq), F32),
                        pltpu.VMEM((att_w // HEAD_DIM, HEAD_DIM + DENOM_ROWS, tq), F32)])
    return pl.pallas_call(
        _attn_kernel,
        grid_spec=grid_spec,
        out_shape=jax.ShapeDtypeStruct((B, T, D), F32),
        compiler_params=pltpu.CompilerParams(vmem_limit_bytes=VMEM_LIMIT),
        name="gqa_outproj",
    )(bounded, qT, k, vT, pool, x, mod, w_out)


def _ffn_kernel(x_ref, xp_ref, xn_ref, mod_ref, g2_ref, wup_ref, cw_ref, cb_ref, wdn_ref, o_ref,
                h_ref, acc_ref, u_ref, y_ref, *, stage_w):
    i = pl.program_id(1)
    n_i = pl.num_programs(1)
    tf = x_ref.shape[1]
    d_ff = wdn_ref.shape[0]

    sh2 = mod_ref[0, 3:4, :]
    sc2 = mod_ref[0, 4:5, :]
    gate2 = mod_ref[0, 5:6, :]
    g2 = g2_ref[...]
    has_prev = (i > 0).astype(F32)
    has_next = (i < n_i - 1).astype(F32)
    h_ref[...] = jnp.concatenate(
        [_rms_modulate(xp_ref[0], g2, sc2, sh2) * has_prev,
         _rms_modulate(x_ref[0], g2, sc2, sh2),
         _rms_modulate(xn_ref[0], g2, sc2, sh2) * has_next], axis=0).astype(BF16)

    half = tf // 2

    def up(slot, s0, w):
        hv = h_ref[...]
        for part, c0 in enumerate((s0, d_ff + s0)):
            u = jnp.dot(hv, wup_ref[:, c0:c0 + w], preferred_element_type=F32)
            for k in range(w // LANES):
                u_ref[slot, part, k] = u[:, k * LANES:(k + 1) * LANES]

    def conv(slot, part, k, c0, parity):
        cs = slice(c0 + k * LANES, c0 + (k + 1) * LANES)
        taps = [u_ref[slot, part, k, pl.ds(HALO - 1 + parity + j, half, stride=2), :] for j in range(CONV_W)]
        y = taps[0] * cw_ref[0:1, cs] + taps[1] * cw_ref[1:2, cs] + taps[2] * cw_ref[2:3, cs]
        return y + cb_ref[:, cs]

    def down(slot, s0, w):
        blocks = []
        for parity in range(2):
            cols = []
            for k in range(w // LANES):
                gate = conv(slot, 0, k, s0, parity)
                val = conv(slot, 1, k, d_ff + s0, parity)
                cols.append((gate * jax.nn.sigmoid(gate) * val).astype(BF16))
            blocks.append(jnp.concatenate(cols, axis=1))
        a = jnp.concatenate(blocks, axis=0)
        part = jnp.dot(a, wdn_ref[s0:s0 + w, :], preferred_element_type=F32)
        if s0 == 0:
            acc_ref[...] = part
        else:
            acc_ref[...] += part

    stages = [(s0, min(stage_w, d_ff - s0)) for s0 in range(0, d_ff, stage_w)]
    pending = None
    for n, (s0, w) in enumerate(stages):
        up(n % 2, s0, w)
        if pending is not None:
            down(*pending)
        pending = (n % 2, s0, w)
    down(*pending)

    for k in range(acc_ref.shape[1] // LANES):
        cs = slice(k * LANES, (k + 1) * LANES)
        for parity in range(2):
            y_ref[k, pl.ds(parity, half, stride=2), :] = acc_ref[parity * half:(parity + 1) * half, cs]
    for k in range(acc_ref.shape[1] // LANES):
        cs = slice(k * LANES, (k + 1) * LANES)
        o_ref[0, :, cs] = x_ref[0, :, cs] + gate2[:, cs] * y_ref[k]


def _ffn(x1, mod, g2, w_up, conv_w, conv_b, w_down, *, tf, stage_w):
    B, T, D = x1.shape
    n_t = T // tf
    d_ff = w_down.shape[0]
    prev_spec, next_spec = _halo_specs(tf, n_t, D)
    const2 = lambda b, i: (0, 0)
    resident = dict(pipeline_mode=pl.Buffered(1))
    kern = functools.partial(_ffn_kernel, stage_w=stage_w)
    return pl.pallas_call(
        kern,
        grid=(B, n_t),
        in_specs=[pl.BlockSpec((1, tf, D), lambda b, i: (b, i, 0)),
                  prev_spec, next_spec,
                  pl.BlockSpec((1, N_MOD, D), lambda b, i: (b, 0, 0)),
                  pl.BlockSpec((1, D), const2),
                  pl.BlockSpec(w_up.shape, const2, **resident),
                  pl.BlockSpec(conv_w.shape, const2),
                  pl.BlockSpec((1, 2 * d_ff), const2),
                  pl.BlockSpec(w_down.shape, const2, **resident)],
        out_specs=pl.BlockSpec((1, tf, D), lambda b, i: (b, i, 0)),
        out_shape=jax.ShapeDtypeStruct((B, T, D), F32),
        scratch_shapes=[pltpu.VMEM((tf + 2 * HALO, D), BF16),
                        pltpu.VMEM((tf, D), F32),
                        pltpu.VMEM((2, 2, stage_w // LANES, tf + 2 * HALO, LANES), F32),
                        pltpu.VMEM((D // LANES, tf, LANES), F32)],
        compiler_params=pltpu.CompilerParams(vmem_limit_bytes=VMEM_LIMIT),
        name="conv_gated_ffn",
    )(x1, x1, x1, mod, g2, w_up, conv_w, conv_b, w_down)


def _rope_tables(T):
    axis_dim = HEAD_DIM // 2
    pos = np.arange(T)
    d = np.arange(HEAD_DIM)
    inv = (1.0 / (ROPE_THETA ** (np.arange(0, axis_dim, 2, dtype=np.float32) / axis_dim))).astype(np.float32)
    freq = inv[d % (axis_dim // 2)]
    p = np.where((d // axis_dim == 0)[:, None], (pos // GRID_W)[None, :], (pos % GRID_W)[None, :]).astype(np.float32)
    ang = p * freq[:, None]
    sign = np.where((d % axis_dim) < axis_dim // 2, -1.0, 1.0).astype(np.float32)
    return jnp.asarray(np.cos(ang), F32), jnp.asarray(np.sin(ang) * sign[:, None], F32)


def _head_gain_rows(gain):
    return jnp.broadcast_to(jnp.tile(gain, LANES // HEAD_DIM)[:, None], (LANES, LANES))


def _scores_bounded(q_gain, k_gain):
    bound = HEAD_DIM * jnp.max(jnp.abs(q_gain)) * jnp.max(jnp.abs(k_gain))
    return (bound <= SCORE_BOUND).astype(jnp.int32).reshape(1)


def kernel(x, c, w_ada, b_ada, norm1_g, w_in, q_norm_g, k_norm_g, w_pool, pool_scale, w_out, norm2_g,
           w_up, conv_w, conv_b, w_down):
    B, T, D = x.shape
    depth = w_ada.shape[0]
    att_w = D // 2
    tile = TOKEN_TILE
    assert T % tile == 0 and T % GRID_W == 0 and D % LANES == 0
    q_scale = math.log2(math.e) / math.sqrt(HEAD_DIM)
    cos_t, sin_t = _rope_tables(T)

    for l in range(depth):
        mod = _modulation(c, w_ada[l], b_ada[l]).reshape(B, N_MOD, D)
        qT, k, vT, pool = _projection(
            x, mod, norm1_g[l].reshape(1, D), w_in[l].astype(BF16),
            _head_gain_rows(q_norm_g[l] * q_scale), _head_gain_rows(k_norm_g[l]),
            cos_t, sin_t, w_pool[l].astype(BF16), pool_scale[l].reshape(1, -1),
            tt=tile, att_w=att_w)
        bounded = _scores_bounded(q_norm_g[l] * q_scale, k_norm_g[l])
        x1 = _attention(bounded, qT, k, vT, pool, x, mod, w_out[l].astype(BF16), tq=tile)
        x = _ffn(x1, mod, norm2_g[l].reshape(1, D), w_up[l].astype(BF16), conv_w[l],
                 conv_b[l].reshape(1, -1), w_down[l].astype(BF16), tf=tile, stage_w=FFN_STAGE)
    return x
```

```python
import functools
import math

import numpy as np
import jax
import jax.numpy as jnp
from jax import lax
from jax.experimental import pallas as pl
from jax.experimental.pallas import tpu as pltpu

HEAD_DIM = 64
N_KV_HEADS = 2
GRID_W = 64
ROPE_THETA = 10000.0
POOL_WINDOWS = (2, 4, 8, 16)
CONV_W = 3
N_MOD = 6
EPS = 1e-6

LANES = 128
HALO = 8
VMEM_LIMIT = 56 * 1024 * 1024
TOKEN_TILE = 512
FFN_STAGE = 512
PROJ_SUBTILES = 4
SCORE_LOOKAHEAD = 2
KEY_STAGE = 256
SCORE_BOUND = 64.0
DENOM_ROWS = 8

F32 = jnp.float32
BF16 = jnp.bfloat16


def _rms_modulate(xv, g, sc, sh):
    ms = jnp.mean(xv * xv, axis=-1, keepdims=True)
    return xv * lax.rsqrt(ms + EPS) * (g * (1.0 + sc)) + sh


def _halo_specs(tile, n_tiles, width):
    per = tile // HALO
    last = n_tiles * per - 1
    prev = pl.BlockSpec((1, HALO, width), lambda b, i: (b, jnp.maximum(i * per - 1, 0), 0))
    nxt = pl.BlockSpec((1, HALO, width), lambda b, i: (b, jnp.minimum((i + 1) * per, last), 0))
    return prev, nxt


def _mod_kernel(c_ref, w_ref, b_ref, o_ref):
    c = c_ref[...]
    c_act = c * jax.nn.sigmoid(c)
    o_ref[...] = jnp.dot(c_act, w_ref[...], preferred_element_type=F32) + b_ref[...]


def _modulation(c, w_ada, b_ada):
    B, D = c.shape
    n = w_ada.shape[1]
    bn = D
    return pl.pallas_call(
        _mod_kernel,
        grid=(n // bn,),
        in_specs=[pl.BlockSpec((B, D), lambda j: (0, 0)),
                  pl.BlockSpec((D, bn), lambda j: (0, j)),
                  pl.BlockSpec((1, bn), lambda j: (0, j))],
        out_specs=pl.BlockSpec((B, bn), lambda j: (0, j)),
        out_shape=jax.ShapeDtypeStruct((B, n), F32),
        compiler_params=pltpu.CompilerParams(vmem_limit_bytes=VMEM_LIMIT),
        name="adaln_modulation",
    )(c, w_ada, b_ada.reshape(1, n))


def _proj_kernel(x_ref, xp_ref, xn_ref, mod_ref, g1_ref, win_ref, gq_ref, gk_ref, cos_ref, sin_ref,
                 wpool_ref, pscale_ref, qT_ref, k_ref, vT_ref, pool_ref, u_ref, y_ref, *, seq_len, sub):
    i = pl.program_id(1)
    n_i = pl.num_programs(1)
    tt = x_ref.shape[1]
    n_sub = tt // sub
    half = sub // 2
    att_w = qT_ref.shape[1]
    kv_w = N_KV_HEADS * HEAD_DIM
    pool_w = pool_ref.shape[2]
    pool0 = att_w + 2 * kv_w
    gdim = pool_w // len(POOL_WINDOWS)

    sh1 = mod_ref[0, 0:1, :]
    sc1 = mod_ref[0, 1:2, :]
    g1 = g1_ref[...]
    has_prev = (i > 0).astype(F32)
    has_next = (i < n_i - 1).astype(F32)

    def normed(xv):
        return _rms_modulate(xv, g1, sc1, sh1)

    def project(t):
        r0 = t * sub
        h = normed(x_ref[0, r0:r0 + sub, :]).astype(BF16)
        before = normed(xp_ref[0]) * has_prev if t == 0 else normed(x_ref[0, r0 - HALO:r0, :])
        after = normed(xn_ref[0]) * has_next if t == n_sub - 1 else normed(x_ref[0, r0 + sub:r0 + sub + HALO, :])
        h_halo = jnp.concatenate([before, after], axis=0).astype(BF16)
        w_pool_in = win_ref[:, pool0:pool0 + pool_w]
        u_main = jnp.dot(h, w_pool_in, preferred_element_type=F32)
        u_halo = jnp.dot(h_halo, w_pool_in, preferred_element_type=F32)
        qkv = jnp.dot(h, win_ref[:, 0:pool0], preferred_element_type=F32)
        for g in range(len(POOL_WINDOWS)):
            cs = slice(g * gdim, (g + 1) * gdim)
            u_ref[t, g, 0:HALO, :] = u_halo[0:HALO, cs]
            u_ref[t, g, HALO:HALO + sub, :] = u_main[:, cs]
            u_ref[t, g, HALO + sub:, :] = u_halo[HALO:, cs]
        return qkv

    def pool_mix(t):
        r0 = t * sub
        j2 = 2 * lax.broadcasted_iota(jnp.int32, (HALO, 1), 0)
        for g, w in enumerate(POOL_WINDOWS):
            assert w & (w - 1) == 0 and w // 2 <= HALO
            cs = slice(g * gdim, (g + 1) * gdim)
            halves = []
            for parity in range(2):
                taps = [u_ref[t, g, pl.ds(HALO + o + parity, half, stride=2), :] for o in range(-w // 2, w // 2)]
                win = taps[0]
                for tap in taps[1:]:
                    win = win + tap
                centre = taps[w // 2]

                def edge(j0):
                    tok = i * tt + r0 + 2 * j0 + j2 + parity
                    lo = jnp.maximum(tok - w // 2, 0)
                    hi = jnp.minimum(tok + w // 2 - 1, seq_len - 1)
                    cnt = (hi - lo + 1).astype(F32)
                    return win[j0:j0 + HALO] / cnt - centre[j0:j0 + HALO]

                inner = win[HALO:half - HALO] * (1.0 / w) - centre[HALO:half - HALO]
                halves.append(jnp.concatenate([edge(0), inner, edge(half - HALO)], axis=0))
            pooled = jnp.concatenate(halves, axis=0).astype(BF16)
            mixed = jnp.dot(pooled, wpool_ref[g], preferred_element_type=F32) * pscale_ref[:, cs]
            for parity in range(2):
                y_ref[t, g, pl.ds(parity, half, stride=2), :] = mixed[parity * half:(parity + 1) * half]
            pool_ref[0, r0:r0 + sub, cs] = y_ref[t, g].astype(BF16)

    n_col = sub // LANES

    def norm_rope_t(blk, gain_ref, cos, sin):
        xt = blk.T
        gain = jnp.concatenate([gain_ref[...]] * n_col, axis=1)
        outs = []
        for hh in range(LANES // HEAD_DIM):
            rows = slice(hh * HEAD_DIM, (hh + 1) * HEAD_DIM)
            xh = xt[rows, :]
            ms = jnp.mean(xh * xh, axis=0, keepdims=True)
            y = xh * lax.rsqrt(ms + EPS) * gain[rows, :]
            q4 = HEAD_DIM // 4
            partner = jnp.concatenate([y[q4:2 * q4], y[0:q4], y[3 * q4:], y[2 * q4:3 * q4]], axis=0)
            outs.append(y * cos + partner * sin)
        return outs

    def qkv_outputs(t, qkv):
        r0 = t * sub
        cos = cos_ref[:, r0:r0 + sub]
        sin = sin_ref[:, r0:r0 + sub]
        for blk in range(att_w // LANES):
            for hh, qt in enumerate(norm_rope_t(qkv[:, blk * LANES:(blk + 1) * LANES], gq_ref, cos, sin)):
                h = blk * (LANES // HEAD_DIM) + hh
                qT_ref[0, h * HEAD_DIM:(h + 1) * HEAD_DIM, r0:r0 + sub] = qt.astype(BF16)
        k_heads = norm_rope_t(qkv[:, att_w:att_w + kv_w], gk_ref, cos, sin)
        k_ref[0, r0:r0 + sub, :] = jnp.concatenate(k_heads, axis=0).T.astype(BF16)
        vT_ref[0, t] = qkv[:, att_w + kv_w:pool0].T.astype(BF16)

    qkvs = [project(t) for t in range(n_sub)]
    for t in range(n_sub):
        pool_mix(t)
    for t in range(n_sub):
        qkv_outputs(t, qkvs[t])


def _projection(x, mod, g1, w_in, gq, gk, cos_t, sin_t, w_pool, pool_scale, *, tt, sub, att_w):
    B, T, D = x.shape
    n_t = T // tt
    in_w = w_in.shape[1]
    kv_w = N_KV_HEADS * HEAD_DIM
    pool_w = in_w - att_w - 2 * kv_w
    prev_spec, next_spec = _halo_specs(tt, n_t, D)
    const2 = lambda b, i: (0, 0)
    n_sub = tt // sub
    kern = functools.partial(_proj_kernel, seq_len=T, sub=sub)
    return pl.pallas_call(
        kern,
        grid=(B, n_t),
        in_specs=[pl.BlockSpec((1, tt, D), lambda b, i: (b, i, 0)),
                  prev_spec, next_spec,
                  pl.BlockSpec((1, N_MOD, D), lambda b, i: (b, 0, 0)),
                  pl.BlockSpec((1, D), const2),
                  pl.BlockSpec((D, in_w), const2),
                  pl.BlockSpec((LANES, LANES), const2),
                  pl.BlockSpec((LANES, LANES), const2),
                  pl.BlockSpec((HEAD_DIM, tt), lambda b, i: (0, i)),
                  pl.BlockSpec((HEAD_DIM, tt), lambda b, i: (0, i)),
                  pl.BlockSpec(w_pool.shape, lambda b, i: (0, 0, 0)),
                  pl.BlockSpec((1, pool_w), const2)],
        out_specs=[pl.BlockSpec((1, att_w, tt), lambda b, i: (b, 0, i)),
                   pl.BlockSpec((1, tt, kv_w), lambda b, i: (b, i, 0)),
                   pl.BlockSpec((1, n_sub, kv_w, sub), lambda b, i: (b, i, 0, 0)),
                   pl.BlockSpec((1, tt, pool_w), lambda b, i: (b, i, 0))],
        out_shape=[jax.ShapeDtypeStruct((B, att_w, T), BF16),
                   jax.ShapeDtypeStruct((B, T, kv_w), BF16),
                   jax.ShapeDtypeStruct((B, n_t * n_sub, kv_w, sub), BF16),
                   jax.ShapeDtypeStruct((B, T, pool_w), BF16)],
        scratch_shapes=[pltpu.VMEM((n_sub, len(POOL_WINDOWS), sub + 2 * HALO, LANES), F32),
                        pltpu.VMEM((n_sub, len(POOL_WINDOWS), sub, LANES), F32)],
        compiler_params=pltpu.CompilerParams(vmem_limit_bytes=VMEM_LIMIT),
        name="proj_qkv_pool",
    )(x, x, x, mod, g1, w_in, gq, gk, cos_t, sin_t, w_pool, pool_scale)


def _attn_kernel(bounded_ref, qT_ref, k_ref, vT_ref, pool_ref, x_ref, mod_ref, wout_ref, x1_ref,
                 oT_ref, acc_ref):
    n_chunks = vT_ref.shape[1]
    tk = vT_ref.shape[3]
    tq = qT_ref.shape[2]
    att_w = qT_ref.shape[1]
    n_heads = att_w // HEAD_DIM
    q_per_kv = n_heads // N_KV_HEADS

    def head_rows(h):
        return slice(h * HEAD_DIM, (h + 1) * HEAD_DIM)

    def padded_q(qh, g):
        zero = jnp.zeros_like(qh)
        return jnp.concatenate([qh if gg == g else zero for gg in range(N_KV_HEADS)], axis=0)

    bounded = bounded_ref[0] == 1

    @pl.when(bounded)
    def _():
        per_chunk = tk // KEY_STAGE

        def scores(c, h):
            ks = k_ref[0, c * KEY_STAGE:(c + 1) * KEY_STAGE, :]
            return jnp.dot(ks, padded_q(qT_ref[0, head_rows(h), :], h // q_per_kv),
                           preferred_element_type=F32)

        def accumulate(c, h, s):
            g = h // q_per_kv
            lanes = slice((c % per_chunk) * KEY_STAGE, (c % per_chunk + 1) * KEY_STAGE)
            p = jnp.exp2(s)
            l = jnp.sum(p, axis=0, keepdims=True)
            pv = jnp.dot(vT_ref[0, c // per_chunk, head_rows(g), lanes], p.astype(BF16), preferred_element_type=F32)
            if c == 0:
                acc_ref[h, 0:HEAD_DIM, :] = pv
                acc_ref[h, HEAD_DIM:HEAD_DIM + 1, :] = l
            else:
                acc_ref[h, 0:HEAD_DIM, :] += pv
                acc_ref[h, HEAD_DIM:HEAD_DIM + 1, :] += l

        stages = [(c, g * q_per_kv + hh) for g in range(N_KV_HEADS) for c in range(n_chunks * per_chunk)
                  for hh in range(q_per_kv)]
        in_flight = []
        for c, h in stages:
            in_flight.append((c, h, scores(c, h)))
            if len(in_flight) > SCORE_LOOKAHEAD:
                accumulate(*in_flight.pop(0))
        for item in in_flight:
            accumulate(*item)
        for h in range(n_heads):
            oT_ref[head_rows(h), :] = acc_ref[h, 0:HEAD_DIM, :] / acc_ref[h, HEAD_DIM:HEAD_DIM + 1, :]

    @pl.when(jnp.logical_not(bounded))
    def _():
        def per_head(h, carry):
            g = h // q_per_kv
            qh = qT_ref[0, pl.ds(pl.multiple_of(h * HEAD_DIM, HEAD_DIM), HEAD_DIM), :]
            zero = jnp.zeros_like(qh)
            qpad = jnp.concatenate([jnp.where(g == gg, qh, zero) for gg in range(N_KV_HEADS)], axis=0)

            def chunk(c, state):
                m, l, acc = state
                ks = k_ref[0, pl.ds(pl.multiple_of(c * tk, tk), tk), :]
                vs = vT_ref[0, c, pl.ds(pl.multiple_of(g * HEAD_DIM, HEAD_DIM), HEAD_DIM), :]
                s = jnp.dot(ks, qpad, preferred_element_type=F32)
                m_new = jnp.maximum(m, jnp.max(s, axis=0, keepdims=True))
                alpha = jnp.exp2(m - m_new)
                p = jnp.exp2(s - m_new)
                l = alpha * l + jnp.sum(p, axis=0, keepdims=True)
                acc = alpha * acc + jnp.dot(vs, p.astype(BF16), preferred_element_type=F32)
                return m_new, l, acc

            init = (jnp.full((1, tq), -jnp.inf, F32), jnp.zeros((1, tq), F32), jnp.zeros((HEAD_DIM, tq), F32))
            _, l, acc = lax.fori_loop(0, n_chunks, chunk, init)
            oT_ref[pl.ds(pl.multiple_of(h * HEAD_DIM, HEAD_DIM), HEAD_DIM), :] = acc / l
            return carry

        lax.fori_loop(0, n_heads, per_head, 0)

    att = oT_ref[...].T.astype(BF16)
    mix = jnp.dot(att, wout_ref[0:att_w, :], preferred_element_type=F32)
    mix = mix + jnp.dot(pool_ref[0], wout_ref[att_w:, :], preferred_element_type=F32)
    g1 = mod_ref[0, 2:3, :]
    x1_ref[0] = x_ref[0] + g1 * mix


def _attention(bounded, qT, k, vT, pool, x, mod, w_out, *, tq):
    B, T, D = x.shape
    att_w = qT.shape[1]
    pool_w = pool.shape[2]
    grid_spec = pltpu.PrefetchScalarGridSpec(
        num_scalar_prefetch=1,
        grid=(B, T // tq),
        in_specs=[pl.BlockSpec((1, att_w, tq), lambda b, i, flag: (b, 0, i)),
                  pl.BlockSpec((1,) + k.shape[1:], lambda b, i, flag: (b, 0, 0)),
                  pl.BlockSpec((1,) + vT.shape[1:], lambda b, i, flag: (b, 0, 0, 0)),
                  pl.BlockSpec((1, tq, pool_w), lambda b, i, flag: (b, i, 0)),
                  pl.BlockSpec((1, tq, D), lambda b, i, flag: (b, i, 0)),
                  pl.BlockSpec((1, N_MOD, D), lambda b, i, flag: (b, 0, 0)),
                  pl.BlockSpec(w_out.shape, lambda b, i, flag: (0, 0))],
        out_specs=pl.BlockSpec((1, tq, D), lambda b, i, flag: (b, i, 0)),
        scratch_shapes=[pltpu.VMEM((att_w, tq), F32),
                        pltpu.VMEM((att_w // HEAD_DIM, HEAD_DIM + DENOM_ROWS, tq), F32)])
    return pl.pallas_call(
        _attn_kernel,
        grid_spec=grid_spec,
        out_shape=jax.ShapeDtypeStruct((B, T, D), F32),
        compiler_params=pltpu.CompilerParams(vmem_limit_bytes=VMEM_LIMIT),
        name="gqa_outproj",
    )(bounded, qT, k, vT, pool, x, mod, w_out)


def _ffn_kernel(x_ref, xp_ref, xn_ref, mod_ref, g2_ref, wup_ref, cw_ref, cb_ref, wdn_ref, o_ref,
                h_ref, acc_ref, u_ref, y_ref, *, stage_w):
    i = pl.program_id(1)
    n_i = pl.num_programs(1)
    tf = x_ref.shape[1]
    d_ff = wdn_ref.shape[0]

    sh2 = mod_ref[0, 3:4, :]
    sc2 = mod_ref[0, 4:5, :]
    gate2 = mod_ref[0, 5:6, :]
    g2 = g2_ref[...]
    has_prev = (i > 0).astype(F32)
    has_next = (i < n_i - 1).astype(F32)
    h_ref[...] = jnp.concatenate(
        [_rms_modulate(xp_ref[0], g2, sc2, sh2) * has_prev,
         _rms_modulate(x_ref[0], g2, sc2, sh2),
         _rms_modulate(xn_ref[0], g2, sc2, sh2) * has_next], axis=0).astype(BF16)

    half = tf // 2

    def up(slot, s0, w):
        hv = h_ref[...]
        for part, c0 in enumerate((s0, d_ff + s0)):
            u = jnp.dot(hv, wup_ref[:, c0:c0 + w], preferred_element_type=F32)
            for k in range(w // LANES):
                u_ref[slot, part, k] = u[:, k * LANES:(k + 1) * LANES]

    def conv(slot, part, k, c0, parity):
        cs = slice(c0 + k * LANES, c0 + (k + 1) * LANES)
        taps = [u_ref[slot, part, k, pl.ds(HALO - 1 + parity + j, half, stride=2), :] for j in range(CONV_W)]
        y = taps[0] * cw_ref[0:1, cs] + taps[1] * cw_ref[1:2, cs] + taps[2] * cw_ref[2:3, cs]
        return y + cb_ref[:, cs]

    def down(slot, s0, w):
        blocks = []
        for parity in range(2):
            cols = []
            for k in range(w // LANES):
                gate = conv(slot, 0, k, s0, parity)
                val = conv(slot, 1, k, d_ff + s0, parity)
                cols.append((gate * jax.nn.sigmoid(gate) * val).astype(BF16))
            blocks.append(jnp.concatenate(cols, axis=1))
        a = jnp.concatenate(blocks, axis=0)
        part = jnp.dot(a, wdn_ref[s0:s0 + w, :], preferred_element_type=F32)
        if s0 == 0:
            acc_ref[...] = part
        else:
            acc_ref[...] += part

    stages = [(s0, min(stage_w, d_ff - s0)) for s0 in range(0, d_ff, stage_w)]
    pending = None
    for n, (s0, w) in enumerate(stages):
        up(n % 2, s0, w)
        if pending is not None:
            down(*pending)
        pending = (n % 2, s0, w)
    down(*pending)

    for k in range(acc_ref.shape[1] // LANES):
        cs = slice(k * LANES, (k + 1) * LANES)
        for parity in range(2):
            y_ref[k, pl.ds(parity, half, stride=2), :] = acc_ref[parity * half:(parity + 1) * half, cs]
    for k in range(acc_ref.shape[1] // LANES):
        cs = slice(k * LANES, (k + 1) * LANES)
        o_ref[0, :, cs] = x_ref[0, :, cs] + gate2[:, cs] * y_ref[k]


def _ffn(x1, mod, g2, w_up, conv_w, conv_b, w_down, *, tf, stage_w):
    B, T, D = x1.shape
    n_t = T // tf
    d_ff = w_down.shape[0]
    prev_spec, next_spec = _halo_specs(tf, n_t, D)
    const2 = lambda b, i: (0, 0)
    resident = dict(pipeline_mode=pl.Buffered(1))
    kern = functools.partial(_ffn_kernel, stage_w=stage_w)
    return pl.pallas_call(
        kern,
        grid=(B, n_t),
        in_specs=[pl.BlockSpec((1, tf, D), lambda b, i: (b, i, 0)),
                  prev_spec, next_spec,
                  pl.BlockSpec((1, N_MOD, D), lambda b, i: (b, 0, 0)),
                  pl.BlockSpec((1, D), const2),
                  pl.BlockSpec(w_up.shape, const2, **resident),
                  pl.BlockSpec(conv_w.shape, const2),
                  pl.BlockSpec((1, 2 * d_ff), const2),
                  pl.BlockSpec(w_down.shape, const2, **resident)],
        out_specs=pl.BlockSpec((1, tf, D), lambda b, i: (b, i, 0)),
        out_shape=jax.ShapeDtypeStruct((B, T, D), F32),
        scratch_shapes=[pltpu.VMEM((tf + 2 * HALO, D), BF16),
                        pltpu.VMEM((tf, D), F32),
                        pltpu.VMEM((2, 2, stage_w // LANES, tf + 2 * HALO, LANES), F32),
                        pltpu.VMEM((D // LANES, tf, LANES), F32)],
        compiler_params=pltpu.CompilerParams(vmem_limit_bytes=VMEM_LIMIT),
        name="conv_gated_ffn",
    )(x1, x1, x1, mod, g2, w_up, conv_w, conv_b, w_down)


def _rope_tables(T):
    axis_dim = HEAD_DIM // 2
    pos = np.arange(T)
    d = np.arange(HEAD_DIM)
    inv = (1.0 / (ROPE_THETA ** (np.arange(0, axis_dim, 2, dtype=np.float32) / axis_dim))).astype(np.float32)
    freq = inv[d % (axis_dim // 2)]
    p = np.where((d // axis_dim == 0)[:, None], (pos // GRID_W)[None, :], (pos % GRID_W)[None, :]).astype(np.float32)
    ang = p * freq[:, None]
    sign = np.where((d % axis_dim) < axis_dim // 2, -1.0, 1.0).astype(np.float32)
    return jnp.asarray(np.cos(ang), F32), jnp.asarray(np.sin(ang) * sign[:, None], F32)


def _head_gain_rows(gain):
    return jnp.broadcast_to(jnp.tile(gain, LANES // HEAD_DIM)[:, None], (LANES, LANES))


def _scores_bounded(q_gain, k_gain):
    bound = HEAD_DIM * jnp.max(jnp.abs(q_gain)) * jnp.max(jnp.abs(k_gain))
    return (bound <= SCORE_BOUND).astype(jnp.int32).reshape(1)


def kernel(x, c, w_ada, b_ada, norm1_g, w_in, q_norm_g, k_norm_g, w_pool, pool_scale, w_out, norm2_g,
           w_up, conv_w, conv_b, w_down):
    B, T, D = x.shape
    depth = w_ada.shape[0]
    att_w = D // 2
    tile = TOKEN_TILE
    assert T % (PROJ_SUBTILES * tile) == 0 and T % GRID_W == 0 and D % LANES == 0
    q_scale = math.log2(math.e) / math.sqrt(HEAD_DIM)
    cos_t, sin_t = _rope_tables(T)

    for l in range(depth):
        mod = _modulation(c, w_ada[l], b_ada[l]).reshape(B, N_MOD, D)
        qT, k, vT, pool = _projection(
            x, mod, norm1_g[l].reshape(1, D), w_in[l].astype(BF16),
            _head_gain_rows(q_norm_g[l] * q_scale), _head_gain_rows(k_norm_g[l]),
            cos_t, sin_t, w_pool[l].astype(BF16), pool_scale[l].reshape(1, -1),
            tt=PROJ_SUBTILES * tile, sub=tile, att_w=att_w)
        bounded = _scores_bounded(q_norm_g[l] * q_scale, k_norm_g[l])
        x1 = _attention(bounded, qT, k, vT, pool, x, mod, w_out[l].astype(BF16), tq=tile)
        x = _ffn(x1, mod, norm2_g[l].reshape(1, D), w_up[l].astype(BF16), conv_w[l],
                 conv_b[l].reshape(1, -1), w_down[l].astype(BF16), tf=tile, stage_w=FFN_STAGE)
    return x
```

```python
import functools
import math

import numpy as np
import jax
import jax.numpy as jnp
from jax import lax
from jax.experimental import pallas as pl
from jax.experimental.pallas import tpu as pltpu

HEAD_DIM = 64
N_KV_HEADS = 2
GRID_W = 64
ROPE_THETA = 10000.0
POOL_WINDOWS = (2, 4, 8, 16)
CONV_W = 3
N_MOD = 6
EPS = 1e-6

LANES = 128
HALO = 8
VMEM_LIMIT = 56 * 1024 * 1024
TOKEN_TILE = 512
FFN_STAGE = 512
PROJ_SUBTILES = 4
SCORE_LOOKAHEAD = 2
KEY_STAGE = 256
SCORE_BOUND = 64.0
DENOM_ROWS = 8

F32 = jnp.float32
BF16 = jnp.bfloat16


def _rms_modulate(xv, g, sc, sh):
    ms = jnp.mean(xv * xv, axis=-1, keepdims=True)
    return xv * lax.rsqrt(ms + EPS) * (g * (1.0 + sc)) + sh


def _halo_specs(tile, n_tiles, width):
    per = tile // HALO
    last = n_tiles * per - 1
    prev = pl.BlockSpec((1, HALO, width), lambda b, i: (b, jnp.maximum(i * per - 1, 0), 0))
    nxt = pl.BlockSpec((1, HALO, width), lambda b, i: (b, jnp.minimum((i + 1) * per, last), 0))
    return prev, nxt


def _mod_kernel(c_ref, w_ref, b_ref, o_ref):
    c = c_ref[...]
    c_act = c * jax.nn.sigmoid(c)
    o_ref[...] = jnp.dot(c_act, w_ref[...], preferred_element_type=F32) + b_ref[...]


def _modulation(c, w_ada, b_ada):
    B, D = c.shape
    n = w_ada.shape[1]
    bn = n // 2 if n % (2 * LANES) == 0 else n
    return pl.pallas_call(
        _mod_kernel,
        grid=(n // bn,),
        in_specs=[pl.BlockSpec((B, D), lambda j: (0, 0)),
                  pl.BlockSpec((D, bn), lambda j: (0, j)),
                  pl.BlockSpec((1, bn), lambda j: (0, j))],
        out_specs=pl.BlockSpec((B, bn), lambda j: (0, j)),
        out_shape=jax.ShapeDtypeStruct((B, n), F32),
        compiler_params=pltpu.CompilerParams(vmem_limit_bytes=VMEM_LIMIT),
        name="adaln_modulation",
    )(c, w_ada, b_ada.reshape(1, n))


def _proj_kernel(x_ref, xp_ref, xn_ref, mod_ref, g1_ref, win_ref, gq_ref, gk_ref, cos_ref, sin_ref,
                 wpool_ref, pscale_ref, qT_ref, k_ref, vT_ref, pool_ref, u_ref, y_ref, *, seq_len, sub):
    i = pl.program_id(1)
    n_i = pl.num_programs(1)
    tt = x_ref.shape[1]
    n_sub = tt // sub
    half = sub // 2
    att_w = qT_ref.shape[1]
    kv_w = N_KV_HEADS * HEAD_DIM
    pool_w = pool_ref.shape[2]
    pool0 = att_w + 2 * kv_w
    gdim = pool_w // len(POOL_WINDOWS)

    sh1 = mod_ref[0, 0:1, :]
    sc1 = mod_ref[0, 1:2, :]
    g1 = g1_ref[...]
    has_prev = (i > 0).astype(F32)
    has_next = (i < n_i - 1).astype(F32)

    def normed(xv):
        return _rms_modulate(xv, g1, sc1, sh1)

    def project(t):
        r0 = t * sub
        h = normed(x_ref[0, r0:r0 + sub, :]).astype(BF16)
        before = normed(xp_ref[0]) * has_prev if t == 0 else normed(x_ref[0, r0 - HALO:r0, :])
        after = normed(xn_ref[0]) * has_next if t == n_sub - 1 else normed(x_ref[0, r0 + sub:r0 + sub + HALO, :])
        h_halo = jnp.concatenate([before, after], axis=0).astype(BF16)
        w_pool_in = win_ref[:, pool0:pool0 + pool_w]
        u_main = jnp.dot(h, w_pool_in, preferred_element_type=F32)
        u_halo = jnp.dot(h_halo, w_pool_in, preferred_element_type=F32)
        qkv = jnp.dot(h, win_ref[:, 0:pool0], preferred_element_type=F32)
        for g in range(len(POOL_WINDOWS)):
            cs = slice(g * gdim, (g + 1) * gdim)
            u_ref[t, g, 0:HALO, :] = u_halo[0:HALO, cs]
            u_ref[t, g, HALO:HALO + sub, :] = u_main[:, cs]
            u_ref[t, g, HALO + sub:, :] = u_halo[HALO:, cs]
        return qkv

    def pool_mix(t):
        r0 = t * sub
        j2 = 2 * lax.broadcasted_iota(jnp.int32, (HALO, 1), 0)
        for g, w in enumerate(POOL_WINDOWS):
            assert w & (w - 1) == 0 and w // 2 <= HALO
            cs = slice(g * gdim, (g + 1) * gdim)
            halves = []
            for parity in range(2):
                taps = [u_ref[t, g, pl.ds(HALO + o + parity, half, stride=2), :] for o in range(-w // 2, w // 2)]
                win = taps[0]
                for tap in taps[1:]:
                    win = win + tap
                centre = taps[w // 2]

                def edge(j0):
                    tok = i * tt + r0 + 2 * j0 + j2 + parity
                    lo = jnp.maximum(tok - w // 2, 0)
                    hi = jnp.minimum(tok + w // 2 - 1, seq_len - 1)
                    cnt = (hi - lo + 1).astype(F32)
                    return win[j0:j0 + HALO] / cnt - centre[j0:j0 + HALO]

                inner = win[HALO:half - HALO] * (1.0 / w) - centre[HALO:half - HALO]
                halves.append(jnp.concatenate([edge(0), inner, edge(half - HALO)], axis=0))
            pooled = jnp.concatenate(halves, axis=0).astype(BF16)
            mixed = jnp.dot(pooled, wpool_ref[g], preferred_element_type=F32) * pscale_ref[:, cs]
            for parity in range(2):
                y_ref[t, g, pl.ds(parity, half, stride=2), :] = mixed[parity * half:(parity + 1) * half]
            pool_ref[0, r0:r0 + sub, cs] = y_ref[t, g].astype(BF16)

    n_col = sub // LANES

    def norm_rope_t(blk, gain_ref, cos, sin):
        xt = blk.T
        gain = jnp.concatenate([gain_ref[...]] * n_col, axis=1)
        outs = []
        for hh in range(LANES // HEAD_DIM):
            rows = slice(hh * HEAD_DIM, (hh + 1) * HEAD_DIM)
            xh = xt[rows, :]
            ms = jnp.mean(xh * xh, axis=0, keepdims=True)
            y = xh * lax.rsqrt(ms + EPS) * gain[rows, :]
            q4 = HEAD_DIM // 4
            partner = jnp.concatenate([y[q4:2 * q4], y[0:q4], y[3 * q4:], y[2 * q4:3 * q4]], axis=0)
            outs.append(y * cos + partner * sin)
        return outs

    def qkv_outputs(t, qkv):
        r0 = t * sub
        cos = cos_ref[:, r0:r0 + sub]
        sin = sin_ref[:, r0:r0 + sub]
        for blk in range(att_w // LANES):
            for hh, qt in enumerate(norm_rope_t(qkv[:, blk * LANES:(blk + 1) * LANES], gq_ref, cos, sin)):
                h = blk * (LANES // HEAD_DIM) + hh
                qT_ref[0, h * HEAD_DIM:(h + 1) * HEAD_DIM, r0:r0 + sub] = qt.astype(BF16)
        k_heads = norm_rope_t(qkv[:, att_w:att_w + kv_w], gk_ref, cos, sin)
        k_ref[0, r0:r0 + sub, :] = jnp.concatenate(k_heads, axis=0).T.astype(BF16)
        vT_ref[0, t] = qkv[:, att_w + kv_w:pool0].T.astype(BF16)

    qkvs = [project(t) for t in range(n_sub)]
    for t in range(n_sub):
        pool_mix(t)
    for t in range(n_sub):
        qkv_outputs(t, qkvs[t])


def _projection(x, mod, g1, w_in, gq, gk, cos_t, sin_t, w_pool, pool_scale, *, tt, sub, att_w):
    B, T, D = x.shape
    n_t = T // tt
    in_w = w_in.shape[1]
    kv_w = N_KV_HEADS * HEAD_DIM
    pool_w = in_w - att_w - 2 * kv_w
    prev_spec, next_spec = _halo_specs(tt, n_t, D)
    const2 = lambda b, i: (0, 0)
    n_sub = tt // sub
    kern = functools.partial(_proj_kernel, seq_len=T, sub=sub)
    return pl.pallas_call(
        kern,
        grid=(B, n_t),
        in_specs=[pl.BlockSpec((1, tt, D), lambda b, i: (b, i, 0)),
                  prev_spec, next_spec,
                  pl.BlockSpec((1, N_MOD, D), lambda b, i: (b, 0, 0)),
                  pl.BlockSpec((1, D), const2),
                  pl.BlockSpec((D, in_w), const2),
                  pl.BlockSpec((LANES, LANES), const2),
                  pl.BlockSpec((LANES, LANES), const2),
                  pl.BlockSpec((HEAD_DIM, tt), lambda b, i: (0, i)),
                  pl.BlockSpec((HEAD_DIM, tt), lambda b, i: (0, i)),
                  pl.BlockSpec(w_pool.shape, lambda b, i: (0, 0, 0)),
                  pl.BlockSpec((1, pool_w), const2)],
        out_specs=[pl.BlockSpec((1, att_w, tt), lambda b, i: (b, 0, i)),
                   pl.BlockSpec((1, tt, kv_w), lambda b, i: (b, i, 0)),
                   pl.BlockSpec((1, n_sub, kv_w, sub), lambda b, i: (b, i, 0, 0)),
                   pl.BlockSpec((1, tt, pool_w), lambda b, i: (b, i, 0))],
        out_shape=[jax.ShapeDtypeStruct((B, att_w, T), BF16),
                   jax.ShapeDtypeStruct((B, T, kv_w), BF16),
                   jax.ShapeDtypeStruct((B, n_t * n_sub, kv_w, sub), BF16),
                   jax.ShapeDtypeStruct((B, T, pool_w), BF16)],
        scratch_shapes=[pltpu.VMEM((n_sub, len(POOL_WINDOWS), sub + 2 * HALO, LANES), F32),
                        pltpu.VMEM((n_sub, len(POOL_WINDOWS), sub, LANES), F32)],
        compiler_params=pltpu.CompilerParams(vmem_limit_bytes=VMEM_LIMIT),
        name="proj_qkv_pool",
    )(x, x, x, mod, g1, w_in, gq, gk, cos_t, sin_t, w_pool, pool_scale)


def _attn_kernel(bounded_ref, qT_ref, k_ref, vT_ref, pool_ref, x_ref, mod_ref, wout_ref, x1_ref,
                 oT_ref, acc_ref):
    n_chunks = vT_ref.shape[1]
    tk = vT_ref.shape[3]
    tq = qT_ref.shape[2]
    att_w = qT_ref.shape[1]
    n_heads = att_w // HEAD_DIM
    q_per_kv = n_heads // N_KV_HEADS

    def head_rows(h):
        return slice(h * HEAD_DIM, (h + 1) * HEAD_DIM)

    def padded_q(qh, g):
        zero = jnp.zeros_like(qh)
        return jnp.concatenate([qh if gg == g else zero for gg in range(N_KV_HEADS)], axis=0)

    bounded = bounded_ref[0] == 1

    @pl.when(bounded)
    def _():
        per_chunk = tk // KEY_STAGE

        def scores(c, h):
            ks = k_ref[0, c * KEY_STAGE:(c + 1) * KEY_STAGE, :]
            return jnp.dot(ks, padded_q(qT_ref[0, head_rows(h), :], h // q_per_kv),
                           preferred_element_type=F32)

        def accumulate(c, h, s):
            g = h // q_per_kv
            lanes = slice((c % per_chunk) * KEY_STAGE, (c % per_chunk + 1) * KEY_STAGE)
            p = jnp.exp2(s)
            l = jnp.sum(p, axis=0, keepdims=True)
            pv = jnp.dot(vT_ref[0, c // per_chunk, head_rows(g), lanes], p.astype(BF16), preferred_element_type=F32)
            if c == 0:
                acc_ref[h, 0:HEAD_DIM, :] = pv
                acc_ref[h, HEAD_DIM:HEAD_DIM + 1, :] = l
            else:
                acc_ref[h, 0:HEAD_DIM, :] += pv
                acc_ref[h, HEAD_DIM:HEAD_DIM + 1, :] += l

        stages = [(c, g * q_per_kv + hh) for g in range(N_KV_HEADS) for c in range(n_chunks * per_chunk)
                  for hh in range(q_per_kv)]
        in_flight = []
        for c, h in stages:
            in_flight.append((c, h, scores(c, h)))
            if len(in_flight) > SCORE_LOOKAHEAD:
                accumulate(*in_flight.pop(0))
        for item in in_flight:
            accumulate(*item)
        for h in range(n_heads):
            oT_ref[head_rows(h), :] = acc_ref[h, 0:HEAD_DIM, :] / acc_ref[h, HEAD_DIM:HEAD_DIM + 1, :]

    @pl.when(jnp.logical_not(bounded))
    def _():
        def per_head(h, carry):
            g = h // q_per_kv
            qh = qT_ref[0, pl.ds(pl.multiple_of(h * HEAD_DIM, HEAD_DIM), HEAD_DIM), :]
            zero = jnp.zeros_like(qh)
            qpad = jnp.concatenate([jnp.where(g == gg, qh, zero) for gg in range(N_KV_HEADS)], axis=0)

            def chunk(c, state):
                m, l, acc = state
                ks = k_ref[0, pl.ds(pl.multiple_of(c * tk, tk), tk), :]
                vs = vT_ref[0, c, pl.ds(pl.multiple_of(g * HEAD_DIM, HEAD_DIM), HEAD_DIM), :]
                s = jnp.dot(ks, qpad, preferred_element_type=F32)
                m_new = jnp.maximum(m, jnp.max(s, axis=0, keepdims=True))
                alpha = jnp.exp2(m - m_new)
                p = jnp.exp2(s - m_new)
                l = alpha * l + jnp.sum(p, axis=0, keepdims=True)
                acc = alpha * acc + jnp.dot(vs, p.astype(BF16), preferred_element_type=F32)
                return m_new, l, acc

            init = (jnp.full((1, tq), -jnp.inf, F32), jnp.zeros((1, tq), F32), jnp.zeros((HEAD_DIM, tq), F32))
            _, l, acc = lax.fori_loop(0, n_chunks, chunk, init)
            oT_ref[pl.ds(pl.multiple_of(h * HEAD_DIM, HEAD_DIM), HEAD_DIM), :] = acc / l
            return carry

        lax.fori_loop(0, n_heads, per_head, 0)

    att = oT_ref[...].T.astype(BF16)
    mix = jnp.dot(att, wout_ref[0:att_w, :], preferred_element_type=F32)
    mix = mix + jnp.dot(pool_ref[0], wout_ref[att_w:, :], preferred_element_type=F32)
    g1 = mod_ref[0, 2:3, :]
    x1_ref[0] = x_ref[0] + g1 * mix


def _attention(bounded, qT, k, vT, pool, x, mod, w_out, *, tq):
    B, T, D = x.shape
    att_w = qT.shape[1]
    pool_w = pool.shape[2]
    grid_spec = pltpu.PrefetchScalarGridSpec(
        num_scalar_prefetch=1,
        grid=(B, T // tq),
        in_specs=[pl.BlockSpec((1, att_w, tq), lambda b, i, flag: (b, 0, i)),
                  pl.BlockSpec((1,) + k.shape[1:], lambda b, i, flag: (b, 0, 0)),
                  pl.BlockSpec((1,) + vT.shape[1:], lambda b, i, flag: (b, 0, 0, 0)),
                  pl.BlockSpec((1, tq, pool_w), lambda b, i, flag: (b, i, 0)),
                  pl.BlockSpec((1, tq, D), lambda b, i, flag: (b, i, 0)),
                  pl.BlockSpec((1, N_MOD, D), lambda b, i, flag: (b, 0, 0)),
                  pl.BlockSpec(w_out.shape, lambda b, i, flag: (0, 0))],
        out_specs=pl.BlockSpec((1, tq, D), lambda b, i, flag: (b, i, 0)),
        scratch_shapes=[pltpu.VMEM((att_w, tq), F32),
                        pltpu.VMEM((att_w // HEAD_DIM, HEAD_DIM + DENOM_ROWS, tq), F32)])
    return pl.pallas_call(
        _attn_kernel,
        grid_spec=grid_spec,
        out_shape=jax.ShapeDtypeStruct((B, T, D), F32),
        compiler_params=pltpu.CompilerParams(vmem_limit_bytes=VMEM_LIMIT),
        name="gqa_outproj",
    )(bounded, qT, k, vT, pool, x, mod, w_out)


def _ffn_kernel(x_ref, xp_ref, xn_ref, mod_ref, g2_ref, wup_ref, cw_ref, cb_ref, wdn_ref, o_ref,
                h_ref, acc_ref, u_ref, y_ref, *, stage_w):
    i = pl.program_id(1)
    n_i = pl.num_programs(1)
    tf = x_ref.shape[1]
    d_ff = wdn_ref.shape[0]

    sh2 = mod_ref[0, 3:4, :]
    sc2 = mod_ref[0, 4:5, :]
    gate2 = mod_ref[0, 5:6, :]
    g2 = g2_ref[...]
    has_prev = (i > 0).astype(F32)
    has_next = (i < n_i - 1).astype(F32)
    h_ref[...] = jnp.concatenate(
        [_rms_modulate(xp_ref[0], g2, sc2, sh2) * has_prev,
         _rms_modulate(x_ref[0], g2, sc2, sh2),
         _rms_modulate(xn_ref[0], g2, sc2, sh2) * has_next], axis=0).astype(BF16)

    half = tf // 2

    def up(slot, s0, w):
        hv = h_ref[...]
        for part, c0 in enumerate((s0, d_ff + s0)):
            u = jnp.dot(hv, wup_ref[:, c0:c0 + w], preferred_element_type=F32)
            for k in range(w // LANES):
                u_ref[slot, part, k] = u[:, k * LANES:(k + 1) * LANES]

    def conv(slot, part, k, c0, parity):
        cs = slice(c0 + k * LANES, c0 + (k + 1) * LANES)
        taps = [u_ref[slot, part, k, pl.ds(HALO - 1 + parity + j, half, stride=2), :] for j in range(CONV_W)]
        y = taps[0] * cw_ref[0:1, cs] + taps[1] * cw_ref[1:2, cs] + taps[2] * cw_ref[2:3, cs]
        return y + cb_ref[:, cs]

    def down(slot, s0, w):
        blocks = []
        for parity in range(2):
            cols = []
            for k in range(w // LANES):
                gate = conv(slot, 0, k, s0, parity)
                val = conv(slot, 1, k, d_ff + s0, parity)
                cols.append((gate * jax.nn.sigmoid(gate) * val).astype(BF16))
            blocks.append(jnp.concatenate(cols, axis=1))
        a = jnp.concatenate(blocks, axis=0)
        part = jnp.dot(a, wdn_ref[s0:s0 + w, :], preferred_element_type=F32)
        if s0 == 0:
            acc_ref[...] = part
        else:
            acc_ref[...] += part

    stages = [(s0, min(stage_w, d_ff - s0)) for s0 in range(0, d_ff, stage_w)]
    pending = None
    for n, (s0, w) in enumerate(stages):
        up(n % 2, s0, w)
        if pending is not None:
            down(*pending)
        pending = (n % 2, s0, w)
    down(*pending)

    for k in range(acc_ref.shape[1] // LANES):
        cs = slice(k * LANES, (k + 1) * LANES)
        for parity in range(2):
            y_ref[k, pl.ds(parity, half, stride=2), :] = acc_ref[parity * half:(parity + 1) * half, cs]
    for k in range(acc_ref.shape[1] // LANES):
        cs = slice(k * LANES, (k + 1) * LANES)
        o_ref[0, :, cs] = x_ref[0, :, cs] + gate2[:, cs] * y_ref[k]


def _ffn(x1, mod, g2, w_up, conv_w, conv_b, w_down, *, tf, stage_w):
    B, T, D = x1.shape
    n_t = T // tf
    d_ff = w_down.shape[0]
    prev_spec, next_spec = _halo_specs(tf, n_t, D)
    const2 = lambda b, i: (0, 0)
    resident = dict(pipeline_mode=pl.Buffered(1))
    kern = functools.partial(_ffn_kernel, stage_w=stage_w)
    return pl.pallas_call(
        kern,
        grid=(B, n_t),
        in_specs=[pl.BlockSpec((1, tf, D), lambda b, i: (b, i, 0)),
                  prev_spec, next_spec,
                  pl.BlockSpec((1, N_MOD, D), lambda b, i: (b, 0, 0)),
                  pl.BlockSpec((1, D), const2),
                  pl.BlockSpec(w_up.shape, const2, **resident),
                  pl.BlockSpec(conv_w.shape, const2),
                  pl.BlockSpec((1, 2 * d_ff), const2),
                  pl.BlockSpec(w_down.shape, const2, **resident)],
        out_specs=pl.BlockSpec((1, tf, D), lambda b, i: (b, i, 0)),
        out_shape=jax.ShapeDtypeStruct((B, T, D), F32),
        scratch_shapes=[pltpu.VMEM((tf + 2 * HALO, D), BF16),
                        pltpu.VMEM((tf, D), F32),
                        pltpu.VMEM((2, 2, stage_w // LANES, tf + 2 * HALO, LANES), F32),
                        pltpu.VMEM((D // LANES, tf, LANES), F32)],
        compiler_params=pltpu.CompilerParams(vmem_limit_bytes=VMEM_LIMIT),
        name="conv_gated_ffn",
    )(x1, x1, x1, mod, g2, w_up, conv_w, conv_b, w_down)


def _rope_tables(T):
    axis_dim = HEAD_DIM // 2
    pos = np.arange(T)
    d = np.arange(HEAD_DIM)
    inv = (1.0 / (ROPE_THETA ** (np.arange(0, axis_dim, 2, dtype=np.float32) / axis_dim))).astype(np.float32)
    freq = inv[d % (axis_dim // 2)]
    p = np.where((d // axis_dim == 0)[:, None], (pos // GRID_W)[None, :], (pos % GRID_W)[None, :]).astype(np.float32)
    ang = p * freq[:, None]
    sign = np.where((d % axis_dim) < axis_dim // 2, -1.0, 1.0).astype(np.float32)
    return jnp.asarray(np.cos(ang), F32), jnp.asarray(np.sin(ang) * sign[:, None], F32)


def _head_gain_rows(gain):
    return jnp.broadcast_to(jnp.tile(gain, LANES // HEAD_DIM)[:, None], (LANES, LANES))


def _scores_bounded(q_gain, k_gain):
    bound = HEAD_DIM * jnp.max(jnp.abs(q_gain)) * jnp.max(jnp.abs(k_gain))
    return (bound <= SCORE_BOUND).astype(jnp.int32).reshape(1)


def kernel(x, c, w_ada, b_ada, norm1_g, w_in, q_norm_g, k_norm_g, w_pool, pool_scale, w_out, norm2_g,
           w_up, conv_w, conv_b, w_down):
    B, T, D = x.shape
    depth = w_ada.shape[0]
    att_w = D // 2
    tile = TOKEN_TILE
    assert T % (PROJ_SUBTILES * tile) == 0 and T % GRID_W == 0 and D % LANES == 0
    q_scale = math.log2(math.e) / math.sqrt(HEAD_DIM)
    cos_t, sin_t = _rope_tables(T)

    for l in range(depth):
        mod = _modulation(c, w_ada[l], b_ada[l]).reshape(B, N_MOD, D)
        qT, k, vT, pool = _projection(
            x, mod, norm1_g[l].reshape(1, D), w_in[l].astype(BF16),
            _head_gain_rows(q_norm_g[l] * q_scale), _head_gain_rows(k_norm_g[l]),
            cos_t, sin_t, w_pool[l].astype(BF16), pool_scale[l].reshape(1, -1),
            tt=PROJ_SUBTILES * tile, sub=tile, att_w=att_w)
        bounded = _scores_bounded(q_norm_g[l] * q_scale, k_norm_g[l])
        x1 = _attention(bounded, qT, k, vT, pool, x, mod, w_out[l].astype(BF16), tq=tile)
        x = _ffn(x1, mod, norm2_g[l].reshape(1, D), w_up[l].astype(BF16), conv_w[l],
                 conv_b[l].reshape(1, -1), w_down[l].astype(BF16), tf=tile, stage_w=FFN_STAGE)
    return x
```
